```python
import math
import jax, jax.numpy as jnp
from jax import lax
import numpy as np

D_MODEL = 1024
BATCH = 4
SEQ = 4096
DEPTH = 4

GRID_W = 64
N_Q_HEADS = 8
N_KV_HEADS = 2
HEAD_DIM = 64
Q_BLOCK = 128
ROPE_THETA = 10000.0
ATTN_WIDTH = N_Q_HEADS * HEAD_DIM
KV_WIDTH = N_KV_HEADS * HEAD_DIM
SGU_GROUPS = 8
SGU_WIDTH = 512
SGU_GROUP_DIM = SGU_WIDTH // SGU_GROUPS
SGU_CHUNK = 128
Q_END = ATTN_WIDTH
K_END = Q_END + KV_WIDTH
V_END = K_END + KV_WIDTH
Z_END = V_END + 2 * SGU_WIDTH
GA_END = Z_END + D_MODEL
IN_COLS = GA_END + D_MODEL
N_EXPERTS = 64
TOP_K = 8
N_GROUPS = 8
TOPK_GROUPS = 4
EXPERT_FF = 256
SHARED_FF = 256
ROUTED_SCALE = 2.5
MOE_BLOCK = 128
ALPHA = (2 * DEPTH) ** 0.25
BETA = (8 * DEPTH) ** -0.25

kernel_name = 'hybrid_gqa_sgu_moe_deepnorm_encoder'


def layer_norm(x, g, b, eps=1e-5):
    xf = x.astype(jnp.float32)
    mu = jnp.mean(xf, -1, keepdims=True)
    var = jnp.mean(jnp.square(xf - mu), -1, keepdims=True)
    return ((xf - mu) * lax.rsqrt(var + eps)).astype(x.dtype) * g + b


def rms_norm(x, g, eps=1e-6):
    xf = x.astype(jnp.float32)
    return (xf * lax.rsqrt(jnp.mean(jnp.square(xf), -1, keepdims=True) + eps)).astype(x.dtype) * g


def axial_rope_tables(seq_len, rows):
    t = jnp.arange(seq_len)
    row = (t // GRID_W - rows // 2).astype(jnp.float32)
    col = (t % GRID_W - GRID_W // 2).astype(jnp.float32)
    n_pairs = HEAD_DIM // 4
    inv_freq = ROPE_THETA ** (-jnp.arange(n_pairs, dtype=jnp.float32) / n_pairs)
    ang = jnp.concatenate([row[:, None] * inv_freq, col[:, None] * inv_freq], -1)
    return jnp.cos(ang), jnp.sin(ang)


def apply_rope(x, cos, sin):
    xf = x.astype(jnp.float32).reshape(*x.shape[:-1], HEAD_DIM // 2, 2)
    x0, x1 = xf[..., 0], xf[..., 1]
    c = cos[None, :, None, :]
    s = sin[None, :, None, :]
    out = jnp.stack([x0 * c - x1 * s, x0 * s + x1 * c], -1)
    return out.reshape(x.shape).astype(x.dtype)


def attention_branch(q, k, v, q_scale, k_scale, cos, sin):
    B, S = q.shape[:2]
    grp = N_Q_HEADS // N_KV_HEADS
    q = apply_rope(rms_norm(q, q_scale), cos, sin)
    k = apply_rope(rms_norm(k, k_scale), cos, sin)
    nqb = S // Q_BLOCK
    qb = q.reshape(B, nqb, Q_BLOCK, N_KV_HEADS, grp, HEAD_DIM).transpose(1, 0, 3, 4, 2, 5)
    kt = k.transpose(0, 2, 1, 3)
    vt = v.transpose(0, 2, 1, 3)
    scale = HEAD_DIM ** -0.5

    def attend(q_blk):
        s = jnp.einsum('bkgqd,bksd->bkgqs', q_blk, kt).astype(jnp.float32) * scale
        p = jax.nn.softmax(s, axis=-1).astype(vt.dtype)
        return jnp.einsum('bkgqs,bksd->bkgqd', p, vt)

    o = lax.map(attend, qb)
    return o.transpose(1, 0, 4, 2, 3, 5).reshape(B, S, ATTN_WIDTH)


def spatial_gating_branch(z, ln_g, ln_b, w_s, b_s):
    B, S, _ = z.shape
    z = jax.nn.gelu(z, approximate=False)
    u, v = z[..., :SGU_WIDTH], z[..., SGU_WIDTH:]
    v = layer_norm(v, ln_g, ln_b)
    vc = v.reshape(B, S // SGU_CHUNK, SGU_CHUNK, SGU_GROUPS, SGU_GROUP_DIM)
    sv = jnp.einsum('gij,bnjgd->bnigd', w_s, vc) + b_s.T[:, :, None]
    return u * sv.reshape(B, S, SGU_WIDTH)


def token_mixer(h, w_in, q_scale, k_scale, sgu_ln_g, sgu_ln_b, w_s, b_s, w_branch_a, w_branch_b, w_out, cos, sin):
    B, S, _ = h.shape
    p = h @ w_in
    q = p[..., :Q_END].reshape(B, S, N_Q_HEADS, HEAD_DIM)
    k = p[..., Q_END:K_END].reshape(B, S, N_KV_HEADS, HEAD_DIM)
    v = p[..., K_END:V_END].reshape(B, S, N_KV_HEADS, HEAD_DIM)
    z = p[..., V_END:Z_END]
    gate_a = p[..., Z_END:GA_END]
    gate_b = p[..., GA_END:]
    o_a = attention_branch(q, k, v, q_scale, k_scale, cos, sin)
    o_b = spatial_gating_branch(z, sgu_ln_g, sgu_ln_b, w_s, b_s)
    y = jax.nn.sigmoid(gate_a) * (o_a @ w_branch_a) + jax.nn.sigmoid(gate_b) * (o_b @ w_branch_b)
    return y @ w_out


def moe_ffn(h, router_w, router_bias, w_gate, w_up, w_down, ws_gate, ws_up, ws_down):
    B, S, D = h.shape
    T = B * S
    xt = h.reshape(T, D)
    scores = jax.nn.sigmoid((xt @ router_w).astype(jnp.float32))
    sel = scores + router_bias.astype(jnp.float32)
    per_group = N_EXPERTS // N_GROUPS
    group_score = lax.top_k(sel.reshape(T, N_GROUPS, per_group), 2)[0].sum(-1)
    _, gidx = lax.top_k(group_score, TOPK_GROUPS)
    gmask = jnp.any(gidx[..., None] == jnp.arange(N_GROUPS), axis=-2)
    emask = jnp.repeat(gmask, per_group, axis=-1)
    _, eidx = lax.top_k(jnp.where(emask, sel, -jnp.inf), TOP_K)
    wts = jnp.take_along_axis(scores, eidx, -1)
    wts = wts / jnp.sum(wts, -1, keepdims=True) * ROUTED_SCALE
    A = T * TOP_K
    e_flat = eidx.reshape(A)
    t_flat = jnp.repeat(jnp.arange(T, dtype=jnp.int32), TOP_K)
    w_flat = wts.reshape(A)
    order = jnp.argsort(e_flat)
    e_s, t_s, w_s = e_flat[order], t_flat[order], w_flat[order]
    counts = jnp.bincount(e_flat, length=N_EXPERTS)
    starts = jnp.cumsum(counts) - counts
    padded = (counts + MOE_BLOCK - 1) // MOE_BLOCK * MOE_BLOCK
    pends = jnp.cumsum(padded)
    pstarts = pends - padded
    dest = pstarts[e_s] + (jnp.arange(A) - starts[e_s])
    n_blocks = A // MOE_BLOCK + N_EXPERTS
    P = n_blocks * MOE_BLOCK
    tok_buf = jnp.full((P,), T, jnp.int32).at[dest].set(t_s.astype(jnp.int32))
    w_buf = jnp.zeros((P,), xt.dtype).at[dest].set(w_s.astype(xt.dtype))
    blk_expert = jnp.minimum(jnp.searchsorted(pends, jnp.arange(n_blocks) * MOE_BLOCK, side='right'), N_EXPERTS - 1)
    x_pad = jnp.concatenate([xt, jnp.zeros((1, D), xt.dtype)], 0)

    def run_block(args):
        e, toks, ws = args
        xb = x_pad[toks]
        hb = jax.nn.silu(xb @ w_gate[e]) * (xb @ w_up[e])
        return (hb @ w_down[e]) * ws[:, None]

    yb = lax.map(run_block, (blk_expert, tok_buf.reshape(n_blocks, MOE_BLOCK), w_buf.reshape(n_blocks, MOE_BLOCK)))
    routed = jax.ops.segment_sum(yb.reshape(P, D), tok_buf, num_segments=T + 1)[:T]
    shared = (jax.nn.silu(xt @ ws_gate) * (xt @ ws_up)) @ ws_down
    return (routed + shared).reshape(B, S, D)


def setup_inputs(seed: int = 0) -> dict:
    key = jax.random.key(seed)
    ks = jax.random.split(key, 26)
    L, D, E = DEPTH, D_MODEL, N_EXPERTS
    f32 = jnp.float32

    def nrm(k, shape, s):
        return jax.random.normal(k, shape, f32) * s

    def gain(k, shape):
        return 1.0 + 0.02 * jax.random.normal(k, shape, f32)

    w_in = nrm(ks[4], (L, D, IN_COLS), D ** -0.5)
    w_in = w_in.at[:, :, K_END:V_END].multiply(BETA)
    return {
        'x': nrm(ks[0], (BATCH, SEQ, D), 1.0),
        'c': nrm(ks[1], (BATCH, D), 1.0),
        'w_ada': nrm(ks[2], (L, D, 6 * D), 0.5 * D ** -0.5),
        'b_ada': nrm(ks[3], (L, 6 * D), 0.02),
        'w_in': w_in,
        'q_scale': gain(ks[5], (L, HEAD_DIM)),
        'k_scale': gain(ks[6], (L, HEAD_DIM)),
        'sgu_ln_g': gain(ks[7], (L, SGU_WIDTH)),
        'sgu_ln_b': nrm(ks[8], (L, SGU_WIDTH), 0.02),
        'w_spatial': nrm(ks[9], (L, SGU_GROUPS, SGU_CHUNK, SGU_CHUNK), SGU_CHUNK ** -0.5),
        'b_spatial': nrm(ks[10], (L, SGU_GROUPS, SGU_CHUNK), 0.02),
        'w_branch_a': nrm(ks[11], (L, ATTN_WIDTH, D), ATTN_WIDTH ** -0.5),
        'w_branch_b': nrm(ks[12], (L, SGU_WIDTH, D), SGU_WIDTH ** -0.5),
        'w_out': nrm(ks[13], (L, D, D), BETA * D ** -0.5),
        'ln1_g': gain(ks[14], (L, D)),
        'ln1_b': nrm(ks[15], (L, D), 0.02),
        'router_w': nrm(ks[16], (L, D, E), D ** -0.5),
        'router_bias': nrm(ks[17], (L, E), 0.01),
        'w_gate': nrm(ks[18], (L, E, D, EXPERT_FF), D ** -0.5),
        'w_up': nrm(ks[19], (L, E, D, EXPERT_FF), D ** -0.5),
        'w_down': nrm(ks[20], (L, E, EXPERT_FF, D), BETA * EXPERT_FF ** -0.5),
        'ws_gate': nrm(ks[21], (L, D, SHARED_FF), D ** -0.5),
        'ws_up': nrm(ks[22], (L, D, SHARED_FF), D ** -0.5),
        'ws_down': nrm(ks[23], (L, SHARED_FF, D), BETA * SHARED_FF ** -0.5),
        'ln2_g': gain(ks[24], (L, D)),
        'ln2_b': nrm(ks[25], (L, D), 0.02),
    }


def reference(x, c, w_ada, b_ada, w_in, q_scale, k_scale, sgu_ln_g, sgu_ln_b, w_spatial, b_spatial,
              w_branch_a, w_branch_b, w_out, ln1_g, ln1_b, router_w, router_bias, w_gate, w_up, w_down,
              ws_gate, ws_up, ws_down, ln2_g, ln2_b):
    B, S, D = x.shape
    rows = S // GRID_W
    cos, sin = axial_rope_tables(S, rows)
    cond = jax.nn.silu(c)
    for l in range(DEPTH):
        mod = cond @ w_ada[l] + b_ada[l]
        shift1, scale1, gate1, shift2, scale2, gate2 = jnp.split(mod[:, None, :], 6, axis=-1)
        h = x * (1 + scale1) + shift1
        y = token_mixer(h, w_in[l], q_scale[l], k_scale[l], sgu_ln_g[l], sgu_ln_b[l], w_spatial[l], b_spatial[l],
                        w_branch_a[l], w_branch_b[l], w_out[l], cos, sin)
        x = layer_norm(ALPHA * x + gate1 * y, ln1_g[l], ln1_b[l])
        h = x * (1 + scale2) + shift2
        y = moe_ffn(h, router_w[l], router_bias[l], w_gate[l], w_up[l], w_down[l], ws_gate[l], ws_up[l], ws_down[l])
        x = layer_norm(ALPHA * x + gate2 * y, ln2_g[l], ln2_b[l])
    return x
```

```python
import functools
import math

import jax
import jax.numpy as jnp
from jax import lax
from jax.experimental import pallas as pl
from jax.experimental.pallas import tpu as pltpu

F32 = jnp.float32
BF16 = jnp.bfloat16
I32 = jnp.int32

GRID_W = 64
N_Q_HEADS = 8
N_KV_HEADS = 2
HEAD_DIM = 64
ROPE_THETA = 10000.0
ATTN_WIDTH = N_Q_HEADS * HEAD_DIM
KV_WIDTH = N_KV_HEADS * HEAD_DIM
SGU_GROUPS = 8
SGU_WIDTH = 512
SGU_GROUP_DIM = SGU_WIDTH // SGU_GROUPS
SGU_CHUNK = 128
N_EXPERTS = 64
TOP_K = 8
N_GROUPS = 8
TOPK_GROUPS = 4
GROUP_SIZE = N_EXPERTS // N_GROUPS
ROUTED_SCALE = 2.5
LN_EPS = 1e-5
RMS_EPS = 1e-6

LANES = 128
SUBLANES = 8
VMEM_LIMIT = 56 * 1024 * 1024

TM_IN = 256
TQ = 256
TM_MIX = 256
TM_DISP = 128
BM_EXP = 256
TM_COMB = 128


def _params(n_axes):
    return pltpu.CompilerParams(
        dimension_semantics=("arbitrary",) * n_axes, vmem_limit_bytes=VMEM_LIMIT)


def _dot(a, b):
    return jnp.dot(a, b, preferred_element_type=F32)


def _dot_nt(a, b):
    return lax.dot_general(a, b, (((1,), (1,)), ((), ())), preferred_element_type=F32)


def _split_bf16(a):
    hi = a.astype(BF16)
    lo = (a - hi.astype(F32)).astype(BF16)
    return hi, lo


def _layer_norm(r, g, b):
    mu = jnp.mean(r, axis=-1, keepdims=True)
    d = r - mu
    var = jnp.mean(d * d, axis=-1, keepdims=True)
    return d * lax.rsqrt(var + LN_EPS) * g + b


def _mod_kernel(c_ref, w_ref, b_ref, o_ref):
    c = c_ref[...]
    cond = c * jax.nn.sigmoid(c)
    o_ref[...] = _dot(cond, w_ref[...]) + b_ref[...]


def _modulation(c, w_ada, b_ada):
    L, D, D6 = w_ada.shape
    B = c.shape[0]
    nj = D6 // D
    return pl.pallas_call(
        _mod_kernel,
        grid=(L, nj),
        in_specs=[
            pl.BlockSpec((B, D), lambda l, j: (0, 0)),
            pl.BlockSpec((None, D, D), lambda l, j: (l, 0, j)),
            pl.BlockSpec((None, 1, D), lambda l, j: (l, 0, j)),
        ],
        out_specs=pl.BlockSpec((None, B, D), lambda l, j: (l, 0, j)),
        out_shape=jax.ShapeDtypeStruct((L, B, D6), F32),
        compiler_params=_params(2),
        name="modulation",
    )(c, w_ada, b_ada.reshape(L, 1, D6))


def _swap_pairs(a):
    n = a.shape[1]
    lane = lax.broadcasted_iota(I32, a.shape, 1)
    nxt = pltpu.roll(a, n - 1, axis=1)
    prv = pltpu.roll(a, 1, axis=1)
    return jnp.where((lane & 1) == 0, nxt, prv)


def _head_rms(p, gsum_ref, width):
    sq = p * p
    hi, lo = _split_bf16(sq)
    g = gsum_ref[0:width, 0:width]
    ss = _dot(hi, g) + _dot(lo, g)
    return lax.rsqrt(ss * (1.0 / HEAD_DIM) + RMS_EPS)


def _inproj_kernel(x_ref, sc_ref, sh_ref, w_ref, cos_ref, sin_ref, qs_ref, ks_ref, gsum_ref,
                   q_ref, kt_ref, vt_ref, r_ref):
    x = x_ref[...]
    h = (x * (1.0 + sc_ref[0]) + sh_ref[0]).astype(BF16)
    cos2 = cos_ref[...]
    sin2 = sin_ref[...]
    pq = _dot(h, w_ref[:, 0:ATTN_WIDTH])
    qn = pq * _head_rms(pq, gsum_ref, ATTN_WIDTH) * qs_ref[...]
    cos8 = jnp.concatenate([cos2] * (ATTN_WIDTH // LANES), axis=1)
    sin8 = jnp.concatenate([sin2] * (ATTN_WIDTH // LANES), axis=1)
    qr = (qn * cos8 + _swap_pairs(qn) * sin8) * (HEAD_DIM ** -0.5)
    for p in range(ATTN_WIDTH // LANES):
        q_ref[p] = qr[:, p * LANES:(p + 1) * LANES].astype(BF16)
    k0 = ATTN_WIDTH
    pk = _dot(h, w_ref[:, k0:k0 + KV_WIDTH])
    kn = pk * _head_rms(pk, gsum_ref, KV_WIDTH) * ks_ref[...]
    kr = kn * cos2 + _swap_pairs(kn) * sin2
    kt_ref[...] = kr.T.astype(BF16)
    v0 = k0 + KV_WIDTH
    pv = _dot(h, w_ref[:, v0:v0 + KV_WIDTH])
    vt_ref[...] = pv.T.astype(BF16)
    r0 = v0 + KV_WIDTH
    n_rest = r_ref.shape[1]
    step = 1024
    for j in range(n_rest // step):
        r_ref[:, j * step:(j + 1) * step] = _dot(
            h, w_ref[:, r0 + j * step:r0 + (j + 1) * step]).astype(BF16)


def _inproj(x2, scale1, shift1, w_in_b, cos2, sin2, qs, ks, gsum, B, S):
    T, D = x2.shape
    tm = min(TM_IN, S)
    npb = S // tm
    n_rest = w_in_b.shape[1] - ATTN_WIDTH - 2 * KV_WIDTH
    nq = ATTN_WIDTH // LANES
    const = lambda i: (0, 0)
    return pl.pallas_call(
        _inproj_kernel,
        grid=(T // tm,),
        in_specs=[
            pl.BlockSpec((tm, D), lambda i: (i, 0)),
            pl.BlockSpec((1, 1, D), lambda i: (i // npb, 0, 0)),
            pl.BlockSpec((1, 1, D), lambda i: (i // npb, 0, 0)),
            pl.BlockSpec(w_in_b.shape, const),
            pl.BlockSpec((tm, LANES), lambda i: (i % npb, 0)),
            pl.BlockSpec((tm, LANES), lambda i: (i % npb, 0)),
            pl.BlockSpec((1, ATTN_WIDTH), const),
            pl.BlockSpec((1, KV_WIDTH), const),
            pl.BlockSpec(gsum.shape, const),
        ],
        out_specs=[
            pl.BlockSpec((nq, tm, LANES), lambda i: (0, i, 0)),
            pl.BlockSpec((None, KV_WIDTH, tm), lambda i: (i // npb, 0, i % npb)),
            pl.BlockSpec((None, KV_WIDTH, tm), lambda i: (i // npb, 0, i % npb)),
            pl.BlockSpec((tm, n_rest), lambda i: (i, 0)),
        ],
        out_shape=[
            jax.ShapeDtypeStruct((nq, T, LANES), BF16),
            jax.ShapeDtypeStruct((B, KV_WIDTH, S), BF16),
            jax.ShapeDtypeStruct((B, KV_WIDTH, S), BF16),
            jax.ShapeDtypeStruct((T, n_rest), BF16),
        ],
        compiler_params=_params(1),
        name="inproj",
    )(x2, scale1, shift1, w_in_b, cos2, sin2, qs, ks, gsum)


def _attn_kernel(q_ref, kt_ref, vt_ref, o_ref):
    q2 = q_ref[...]
    kt = kt_ref[...]
    vt = vt_ref[...]
    outs = []
    for hh in range(LANES // HEAD_DIM):
        qh = q2[:, hh * HEAD_DIM:(hh + 1) * HEAD_DIM]
        s = _dot(qh, kt)
        m = jnp.max(s, axis=-1, keepdims=True)
        p = jnp.exp(s - m)
        l = jnp.sum(p, axis=-1, keepdims=True)
        o = _dot_nt(p.astype(BF16), vt)
        outs.append(o / l)
    o_ref[...] = jnp.concatenate(outs, axis=1).astype(BF16)


def _attention(q4, kt, vt, B, S):
    nq, T, _ = q4.shape
    tq = min(TQ, S)
    nqb = S // tq
    pairs_per_kv = nq // N_KV_HEADS
    return pl.pallas_call(
        _attn_kernel,
        grid=(B, N_KV_HEADS, nqb, pairs_per_kv),
        in_specs=[
            pl.BlockSpec((None, tq, LANES), lambda b, g, i, p: (g * pairs_per_kv + p, b * nqb + i, 0)),
            pl.BlockSpec((None, HEAD_DIM, S), lambda b, g, i, p: (b, g, 0)),
            pl.BlockSpec((None, HEAD_DIM, S), lambda b, g, i, p: (b, g, 0)),
        ],
        out_specs=pl.BlockSpec((tq, LANES), lambda b, g, i, p: (b * nqb + i, g * pairs_per_kv + p)),
        out_shape=jax.ShapeDtypeStruct((T, ATTN_WIDTH), BF16),
        compiler_params=_params(4),
        name="attention",
    )(q4, kt, vt)


def _gelu(a):
    return 0.5 * a * (1.0 + lax.erf(a * (2.0 ** -0.5)))


def _route(sel_t, scores_t, prefix_t):
    tm = sel_t.shape[1]
    neg = -jnp.inf
    row8 = lax.broadcasted_iota(I32, (GROUP_SIZE, tm), 0)
    groups = [sel_t[g * GROUP_SIZE:(g + 1) * GROUP_SIZE, :] for g in range(N_GROUPS)]
    gscore = []
    for a in groups:
        m1 = jnp.max(a, axis=0, keepdims=True)
        i1 = jnp.min(jnp.where(a == m1, row8, GROUP_SIZE), axis=0, keepdims=True)
        m2 = jnp.max(jnp.where(row8 == i1, neg, a), axis=0, keepdims=True)
        gscore.append(m1 + m2)
    masked = []
    for g in range(N_GROUPS):
        rank = jnp.zeros((1, tm), I32)
        for g2 in range(N_GROUPS):
            if g2 == g:
                continue
            if g2 < g:
                ahead = gscore[g2] >= gscore[g]
            else:
                ahead = gscore[g2] > gscore[g]
            rank = rank + ahead.astype(I32)
        keep = rank < TOPK_GROUPS
        masked.append(jnp.where(keep, groups[g], neg))
    cur = jnp.concatenate(masked, axis=0)
    row = lax.broadcasted_iota(I32, (N_EXPERTS, tm), 0)
    idxs, wts, poss = [], [], []
    chosen = jnp.zeros((N_EXPERTS, tm), F32)
    for _ in range(TOP_K):
        m = jnp.max(cur, axis=0, keepdims=True)
        idx = jnp.min(jnp.where(cur == m, row, N_EXPERTS), axis=0, keepdims=True)
        hit = row == idx
        idxs.append(idx)
        wts.append(jnp.sum(jnp.where(hit, scores_t, 0.0), axis=0, keepdims=True))
        poss.append(hit)
        chosen = jnp.where(hit, 1.0, chosen)
        cur = jnp.where(hit, neg, cur)
    return idxs, wts, poss, chosen


def _mix_kernel(z_ref, ga_ref, gb_ref, oa_ref, x_ref, g1_ref, sc2_ref, sh2_ref,
                sg_ref, sb_ref, wcat_ref, bexp_ref, wa_ref, wb_ref, wo_ref,
                l1g_ref, l1b_ref, rwh_ref, rwl_ref, rb_ref, tri_ref,
                x1_ref, h2_ref, eidx_ref, wts_ref, pos_ref, cnt_ref,
                run_ref, *, alpha):
    i = pl.program_id(0)

    @pl.when(i == 0)
    def _():
        run_ref[...] = jnp.zeros_like(run_ref)

    tm = z_ref.shape[0]
    z = _gelu(z_ref[...].astype(F32))
    u = z[:, :SGU_WIDTH]
    v = _layer_norm(z[:, SGU_WIDTH:], sg_ref[...], sb_ref[...])
    lane = lax.broadcasted_iota(I32, (SGU_CHUNK, SGU_WIDTH), 1)
    wcat = wcat_ref[...]
    svs = []
    for c in range(tm // SGU_CHUNK):
        vc = v[c * SGU_CHUNK:(c + 1) * SGU_CHUNK, :]
        stacked = jnp.concatenate(
            [jnp.where(lane // SGU_GROUP_DIM == g, vc, 0.0) for g in range(SGU_GROUPS)],
            axis=0).astype(BF16)
        svs.append(_dot(wcat, stacked) + bexp_ref[...])
    sv = jnp.concatenate(svs, axis=0) if len(svs) > 1 else svs[0]
    ob = (u * sv).astype(BF16)
    ya = _dot(oa_ref[...], wa_ref[...])
    yb = _dot(ob, wb_ref[...])
    y = (jax.nn.sigmoid(ga_ref[...].astype(F32)) * ya
         + jax.nn.sigmoid(gb_ref[...].astype(F32)) * yb)
    out = _dot(y.astype(BF16), wo_ref[...])
    x1 = _layer_norm(alpha * x_ref[...] + g1_ref[0] * out, l1g_ref[...], l1b_ref[...])
    x1_ref[...] = x1
    h2 = x1 * (1.0 + sc2_ref[0]) + sh2_ref[0]
    h2_ref[...] = h2
    hh, hl = _split_bf16(h2)
    rwh = rwh_ref[...]
    logits_t = _dot_nt(rwh, hh) + _dot_nt(rwh, hl) + _dot_nt(rwl_ref[...], hh)
    scores_t = jax.nn.sigmoid(logits_t)
    sel_t = scores_t + rb_ref[:, 0:1]
    idxs, wts, hits, chosen = _route(sel_t, scores_t, None)
    wsum = wts[0]
    for w in wts[1:]:
        wsum = wsum + w
    inv = ROUTED_SCALE / wsum
    prefix = _dot(chosen.astype(BF16), tri_ref[...]) + run_ref[:, 0:1]
    poss = [jnp.sum(jnp.where(h, prefix, 0.0), axis=0, keepdims=True) for h in hits]
    eidx_ref[...] = jnp.concatenate(idxs, axis=0)
    wts_ref[...] = jnp.concatenate([w * inv for w in wts], axis=0)
    pos_ref[...] = jnp.concatenate(poss, axis=0).astype(I32)
    run_new = run_ref[...] + jnp.sum(chosen, axis=1, keepdims=True)
    run_ref[...] = run_new
    cnt_ref[...] = run_new.astype(I32)


def _mix(rest, oa, x2, gate1, scale2, shift2, sg, sb, wcat, bexp, wa, wb, wo, l1g, l1b,
         rwh, rwl, rb, tri, B, S, alpha):
    T, D = x2.shape
    tm = min(TM_MIX, S)
    npb = S // tm
    const = lambda i: (0, 0)
    modspec = pl.BlockSpec((1, 1, D), lambda i: (i // npb, 0, 0))
    full = lambda a: pl.BlockSpec(a.shape, const)
    return pl.pallas_call(
        functools.partial(_mix_kernel, alpha=alpha),
        grid=(T // tm,),
        in_specs=[
            pl.BlockSpec((tm, D), lambda i: (i, 0)),
            pl.BlockSpec((tm, D), lambda i: (i, 1)),
            pl.BlockSpec((tm, D), lambda i: (i, 2)),
            pl.BlockSpec((tm, ATTN_WIDTH), lambda i: (i, 0)),
            pl.BlockSpec((tm, D), lambda i: (i, 0)),
            modspec, modspec, modspec,
            full(sg), full(sb), full(wcat), full(bexp), full(wa), full(wb), full(wo),
            full(l1g), full(l1b), full(rwh), full(rwl), full(rb), full(tri),
        ],
        out_specs=[
            pl.BlockSpec((tm, D), lambda i: (i, 0)),
            pl.BlockSpec((tm, D), lambda i: (i, 0)),
            pl.BlockSpec((TOP_K, tm), lambda i: (0, i)),
            pl.BlockSpec((TOP_K, tm), lambda i: (0, i)),
            pl.BlockSpec((TOP_K, tm), lambda i: (0, i)),
            pl.BlockSpec((N_EXPERTS, LANES), const),
        ],
        out_shape=[
            jax.ShapeDtypeStruct((T, D), F32),
            jax.ShapeDtypeStruct((T, D), F32),
            jax.ShapeDtypeStruct((TOP_K, T), I32),
            jax.ShapeDtypeStruct((TOP_K, T), F32),
            jax.ShapeDtypeStruct((TOP_K, T), I32),
            jax.ShapeDtypeStruct((N_EXPERTS, LANES), I32),
        ],
        scratch_shapes=[pltpu.VMEM((N_EXPERTS, LANES), F32)],
        compiler_params=_params(1),
        name="mix",
    )(rest, rest, rest, oa, x2, gate1, scale2, shift2, sg, sb, wcat, bexp, wa, wb, wo,
      l1g, l1b, rwh, rwl, rb, tri)


def _row_copy(src_ref, s, dst_ref, d, sem):
    return pltpu.make_async_copy(src_ref.at[pl.ds(s, 1)], dst_ref.at[pl.ds(d, 1)], sem)


def _dispatch_kernel(dest_ref, fill_ref, h_ref, zero_ref, xs_ref, sem):
    i = pl.program_id(0)
    tm = h_ref.shape[0]
    zrows = zero_ref.shape[0]

    def fill_copy(e):
        start = pl.multiple_of(fill_ref[e], SUBLANES)
        return pltpu.make_async_copy(zero_ref, xs_ref.at[pl.ds(start, zrows)], sem)

    @pl.when(i == 0)
    def _():
        def fill(e, c):
            fill_copy(e).start()
            return c
        lax.fori_loop(0, N_EXPERTS, fill, 0)

        def drain(e, c):
            fill_copy(e).wait()
            return c
        lax.fori_loop(0, N_EXPERTS, drain, 0)

    def issue(t, c):
        for k in range(TOP_K):
            _row_copy(h_ref, t, xs_ref, dest_ref[t * TOP_K + k], sem).start()
        return c
    lax.fori_loop(0, tm, issue, 0)

    def drain_rows(t, c):
        for k in range(TOP_K):
            _row_copy(h_ref, 0, xs_ref, 0, sem).wait()
        return c
    lax.fori_loop(0, tm, drain_rows, 0)


def _dispatch(dest_flat, fill_start, h2, n_rows):
    T, D = h2.shape
    tm = min(TM_DISP, T)
    zero_blk = jnp.zeros((BM_EXP + SUBLANES, D), F32)
    return pl.pallas_call(
        _dispatch_kernel,
        grid=(T // tm,),
        in_specs=[
            pl.BlockSpec((tm * TOP_K,), lambda i: (i,), memory_space=pltpu.SMEM),
            pl.BlockSpec((N_EXPERTS,), lambda i: (0,), memory_space=pltpu.SMEM),
            pl.BlockSpec((tm, D), lambda i: (i, 0)),
            pl.BlockSpec(zero_blk.shape, lambda i: (0, 0)),
        ],
        out_specs=pl.BlockSpec(memory_space=pl.ANY),
        out_shape=jax.ShapeDtypeStruct((n_rows, D), F32),
        scratch_shapes=[pltpu.SemaphoreType.DMA(())],
        compiler_params=_params(1),
        name="dispatch",
    )(dest_flat, fill_start, h2, zero_blk)


def _experts_kernel(be_ref, nused_ref, xs_ref, wg_ref, wu_ref, wd_ref, ys_ref,
                    wg_s, wu_s, wd_s):
    n = pl.program_id(0)

    @pl.when(n < nused_ref[0])
    def _():
        prev = be_ref[jnp.maximum(n - 1, 0)]
        first = jnp.logical_or(n == 0, be_ref[n] != prev)

        @pl.when(first)
        def _():
            wg_s[...] = wg_ref[...].astype(BF16)
            wu_s[...] = wu_ref[...].astype(BF16)
            wd_s[...] = wd_ref[...].astype(BF16)

        xb = xs_ref[...].astype(BF16)
        g = _dot(xb, wg_s[...])
        u = _dot(xb, wu_s[...])
        hb = (g * jax.nn.sigmoid(g) * u).astype(BF16)
        ys_ref[...] = _dot(hb, wd_s[...])


def _experts(blk_expert, n_used, xs, w_gate, w_up, w_down, layer, n_blocks):
    _, E, D, FF = w_gate.shape
    bm = BM_EXP

    def row_map(n, be, nu):
        return (jnp.minimum(n, nu[0] - 1), 0)

    grid_spec = pltpu.PrefetchScalarGridSpec(
        num_scalar_prefetch=2,
        grid=(n_blocks,),
        in_specs=[
            pl.BlockSpec((bm, D), row_map),
            pl.BlockSpec((None, None, D, FF), lambda n, be, nu: (layer, be[n], 0, 0)),
            pl.BlockSpec((None, None, D, FF), lambda n, be, nu: (layer, be[n], 0, 0)),
            pl.BlockSpec((None, None, FF, D), lambda n, be, nu: (layer, be[n], 0, 0)),
        ],
        out_specs=pl.BlockSpec((bm, D), row_map),
        scratch_shapes=[
            pltpu.VMEM((D, FF), BF16),
            pltpu.VMEM((D, FF), BF16),
            pltpu.VMEM((FF, D), BF16),
        ],
    )
    return pl.pallas_call(
        _experts_kernel,
        grid_spec=grid_spec,
        out_shape=jax.ShapeDtypeStruct((n_blocks * bm, D), F32),
        compiler_params=_params(1),
        name="experts",
    )(blk_expert, n_used, xs, w_gate, w_up, w_down)


def _combine_kernel(dest_ref, ys_ref, w_ref, h_ref, x1_ref, g2_ref, wsg_ref, wsu_ref, wsd_ref,
                    l2g_ref, l2b_ref, o_ref, gbuf, sem, *, alpha):
    tm = h_ref.shape[0]

    def issue(t, c):
        for k in range(TOP_K):
            pltpu.make_async_copy(ys_ref.at[pl.ds(dest_ref[t * TOP_K + k], 1)],
                                  gbuf.at[k, pl.ds(t, 1)], sem).start()
        return c
    lax.fori_loop(0, tm, issue, 0)

    hb = h_ref[...].astype(BF16)
    g = _dot(hb, wsg_ref[...])
    u = _dot(hb, wsu_ref[...])
    y = _dot((g * jax.nn.sigmoid(g) * u).astype(BF16), wsd_ref[...])

    def drain(t, c):
        for k in range(TOP_K):
            pltpu.make_async_copy(ys_ref.at[pl.ds(0, 1)], gbuf.at[k, pl.ds(0, 1)], sem).wait()
        return c
    lax.fori_loop(0, tm, drain, 0)

    w = w_ref[...]
    for k in range(TOP_K):
        y = y + w[:, k:k + 1] * gbuf[k]
    o_ref[...] = _layer_norm(alpha * x1_ref[...] + g2_ref[0] * y, l2g_ref[...], l2b_ref[...])


def _combine(dest_flat, ys, w_tk, h2, x1, gate2, wsg, wsu, wsd, l2g, l2b, B, S, alpha):
    T, D = h2.shape
    tm = min(TM_COMB, S)
    npb = S // tm
    const = lambda i: (0, 0)
    full = lambda a: pl.BlockSpec(a.shape, const)
    return pl.pallas_call(
        functools.partial(_combine_kernel, alpha=alpha),
        grid=(T // tm,),
        in_specs=[
            pl.BlockSpec((tm * TOP_K,), lambda i: (i,), memory_space=pltpu.SMEM),
            pl.BlockSpec(memory_space=pl.ANY),
            pl.BlockSpec((tm, TOP_K), lambda i: (i, 0)),
            pl.BlockSpec((tm, D), lambda i: (i, 0)),
            pl.BlockSpec((tm, D), lambda i: (i, 0)),
            pl.BlockSpec((1, 1, D), lambda i: (i // npb, 0, 0)),
            full(wsg), full(wsu), full(wsd), full(l2g), full(l2b),
        ],
        out_specs=pl.BlockSpec((tm, D), lambda i: (i, 0)),
        out_shape=jax.ShapeDtypeStruct((T, D), F32),
        scratch_shapes=[pltpu.VMEM((TOP_K, tm, D), F32), pltpu.SemaphoreType.DMA(())],
        compiler_params=_params(1),
        name="combine",
    )(dest_flat, ys, w_tk, h2, x1, gate2, wsg, wsu, wsd, l2g, l2b)


def _rope_tables(S):
    rows = S // GRID_W
    t = jnp.arange(S)
    row = (t // GRID_W - rows // 2).astype(F32)
    col = (t % GRID_W - GRID_W // 2).astype(F32)
    n_pairs = HEAD_DIM // 4
    inv_freq = ROPE_THETA ** (-jnp.arange(n_pairs, dtype=F32) / n_pairs)
    ang = jnp.concatenate([row[:, None] * inv_freq, col[:, None] * inv_freq], -1)
    cos, sin = jnp.cos(ang), jnp.sin(ang)
    cos_i = jnp.repeat(cos, 2, axis=-1)
    sin_i = jnp.stack([-sin, sin], axis=-1).reshape(S, HEAD_DIM)
    reps = LANES // HEAD_DIM
    return jnp.tile(cos_i, (1, reps)), jnp.tile(sin_i, (1, reps))


def kernel(x, c, w_ada, b_ada, w_in, q_scale, k_scale, sgu_ln_g, sgu_ln_b, w_spatial, b_spatial,
           w_branch_a, w_branch_b, w_out, ln1_g, ln1_b, router_w, router_bias, w_gate, w_up, w_down,
           ws_gate, ws_up, ws_down, ln2_g, ln2_b):
    B, S, D = x.shape
    T = B * S
    L = w_ada.shape[0]
    alpha = (2 * L) ** 0.25
    A = T * TOP_K
    n_blocks = A // BM_EXP + N_EXPERTS
    n_rows = (n_blocks + 2) * BM_EXP

    cos2, sin2 = _rope_tables(S)
    mod = _modulation(c, w_ada, b_ada)
    hd = jnp.arange(ATTN_WIDTH) // HEAD_DIM
    gsum = (hd[:, None] == hd[None, :]).astype(BF16)
    tm_mix = min(TM_MIX, S)
    tri = (jnp.arange(tm_mix)[:, None] < jnp.arange(tm_mix)[None, :]).astype(BF16)

    x2 = x.reshape(T, D)
    for l in range(L):
        m6 = mod[l].reshape(B, 1, 6, D)
        shift1, scale1, gate1, shift2, scale2, gate2 = [m6[:, :, j, :] for j in range(6)]
        qs = jnp.tile(q_scale[l], N_Q_HEADS).reshape(1, ATTN_WIDTH)
        ks = jnp.tile(k_scale[l], N_KV_HEADS).reshape(1, KV_WIDTH)
        q4, kt, vt, rest = _inproj(x2, scale1, shift1, w_in[l].astype(BF16), cos2, sin2, qs, ks,
                                   gsum, B, S)
        oa = _attention(q4, kt, vt, B, S)

        wcat = jnp.transpose(w_spatial[l], (1, 0, 2)).reshape(SGU_CHUNK, SGU_GROUPS * SGU_CHUNK)
        bexp = jnp.repeat(b_spatial[l].T, SGU_GROUP_DIM, axis=1)
        rw_t = router_w[l].T
        rwh = rw_t.astype(BF16)
        rwl = (rw_t - rwh.astype(F32)).astype(BF16)
        rb = jnp.broadcast_to(router_bias[l].astype(F32)[:, None], (N_EXPERTS, LANES))
        x1, h2, eidx_t, wts_t, pos_t, cnt = _mix(
            rest, oa, x2, gate1, scale2, shift2,
            sgu_ln_g[l].reshape(1, -1), sgu_ln_b[l].reshape(1, -1), wcat.astype(BF16), bexp,
            w_branch_a[l].astype(BF16), w_branch_b[l].astype(BF16), w_out[l].astype(BF16),
            ln1_g[l].reshape(1, -1), ln1_b[l].reshape(1, -1), rwh, rwl, rb, tri, B, S, alpha)

        counts = cnt[:, 0]
        padded = (counts + BM_EXP - 1) // BM_EXP * BM_EXP
        pends = jnp.cumsum(padded)
        pstarts = pends - padded
        dest_flat = (pstarts[eidx_t] + pos_t).T.reshape(A).astype(I32)
        fill_start = ((pstarts + counts) // SUBLANES * SUBLANES).astype(I32)
        n_used = (pends[-1] // BM_EXP).astype(I32).reshape(1)
        blk = jnp.minimum(jnp.arange(n_blocks, dtype=I32), n_used[0] - 1)
        blk_expert = jnp.minimum(
            jnp.searchsorted(pends, blk * BM_EXP, side='right'), N_EXPERTS - 1).astype(I32)

        xs = _dispatch(dest_flat, fill_start, h2, n_rows)
        ys = _experts(blk_expert, n_used, xs, w_gate, w_up, w_down, l, n_blocks)
        x2 = _combine(dest_flat, ys, wts_t.T, h2, x1, gate2,
                      ws_gate[l].astype(BF16), ws_up[l].astype(BF16), ws_down[l].astype(BF16),
                      ln2_g[l].reshape(1, -1), ln2_b[l].reshape(1, -1), B, S, alpha)
    return x2.reshape(B, S, D)
```

```python
import functools

import jax
import jax.numpy as jnp
from jax import lax
from jax.experimental import pallas as pl
from jax.experimental.pallas import tpu as pltpu

F32 = jnp.float32
BF16 = jnp.bfloat16
I32 = jnp.int32
U32 = jnp.uint32

GRID_W = 64
N_Q_HEADS = 8
N_KV_HEADS = 2
HEAD_DIM = 64
ROPE_THETA = 10000.0
ATTN_WIDTH = N_Q_HEADS * HEAD_DIM
KV_WIDTH = N_KV_HEADS * HEAD_DIM
SGU_GROUPS = 8
SGU_WIDTH = 512
SGU_GROUP_DIM = SGU_WIDTH // SGU_GROUPS
SGU_CHUNK = 128
N_EXPERTS = 64
TOP_K = 8
N_GROUPS = 8
TOPK_GROUPS = 4
GROUP_SIZE = N_EXPERTS // N_GROUPS
ROUTED_SCALE = 2.5
LN_EPS = 1e-5
RMS_EPS = 1e-6

LANES = 128
SUBLANES = 8
VMEM_LIMIT = 56 * 1024 * 1024

TM_IN = 256
TQ = 256
TM_ROUTE = 256
BM_EXP = 256


def _params(n_axes):
    return pltpu.CompilerParams(
        dimension_semantics=("arbitrary",) * n_axes, vmem_limit_bytes=VMEM_LIMIT)


def _dot(a, b):
    return jnp.dot(a, b, preferred_element_type=F32)


def _dot_nt(a, b):
    return lax.dot_general(a, b, (((1,), (1,)), ((), ())), preferred_element_type=F32)


def _split_bf16(a):
    hi = a.astype(BF16)
    lo = (a - hi.astype(F32)).astype(BF16)
    return hi, lo


def _layer_norm(r, g, b):
    mu = jnp.mean(r, axis=-1, keepdims=True)
    d = r - mu
    var = jnp.mean(d * d, axis=-1, keepdims=True)
    return d * lax.rsqrt(var + LN_EPS) * g + b


def _pack_pairs(y):
    n = y.shape[1] // 2
    hi = lax.bitcast_convert_type(y[:, :n].astype(BF16).astype(F32), U32)
    lo = lax.bitcast_convert_type(y[:, n:].astype(BF16).astype(F32), U32)
    return hi | (lo >> 16)


def _unpack_pairs(w):
    hi = lax.bitcast_convert_type(w & jnp.uint32(0xFFFF0000), F32).astype(BF16)
    lo = lax.bitcast_convert_type(w << 16, F32).astype(BF16)
    return jnp.concatenate([hi, lo], axis=1)


def _mod_kernel(c_ref, w_ref, b_ref, o_ref):
    c = c_ref[...]
    cond = c * jax.nn.sigmoid(c)
    o_ref[...] = _dot(cond, w_ref[...]) + b_ref[...]


def _modulation(c, w_ada, b_ada):
    L, D, D6 = w_ada.shape
    B = c.shape[0]
    nj = D6 // D
    return pl.pallas_call(
        _mod_kernel,
        grid=(L, nj),
        in_specs=[
            pl.BlockSpec((B, D), lambda l, j: (0, 0)),
            pl.BlockSpec((None, D, D), lambda l, j: (l, 0, j)),
            pl.BlockSpec((None, 1, D), lambda l, j: (l, 0, j)),
        ],
        out_specs=pl.BlockSpec((None, B, D), lambda l, j: (l, 0, j)),
        out_shape=jax.ShapeDtypeStruct((L, B, D6), F32),
        compiler_params=_params(2),
        name="modulation",
    )(c, w_ada, b_ada.reshape(L, 1, D6))


def _swap_pairs(a):
    n = a.shape[1]
    lane = lax.broadcasted_iota(I32, a.shape, 1)
    nxt = pltpu.roll(a, n - 1, axis=1)
    prv = pltpu.roll(a, 1, axis=1)
    return jnp.where((lane & 1) == 0, nxt, prv)


def _head_rms(p, gsum_ref, width):
    sq = p * p
    hi, lo = _split_bf16(sq)
    g = gsum_ref[0:width, 0:width]
    ss = _dot(hi, g) + _dot(lo, g)
    return lax.rsqrt(ss * (1.0 / HEAD_DIM) + RMS_EPS)


def _inproj_kernel(x_ref, sc_ref, sh_ref, w_ref, cos_ref, sin_ref, qs_ref, ks_ref, gsum_ref,
                   q_ref, kt_ref, vt_ref, r_ref):
    x = x_ref[...]
    h = (x * (1.0 + sc_ref[0]) + sh_ref[0]).astype(BF16)
    cos2 = cos_ref[...]
    sin2 = sin_ref[...]
    pq = _dot(h, w_ref[:, 0:ATTN_WIDTH])
    qn = pq * _head_rms(pq, gsum_ref, ATTN_WIDTH) * qs_ref[...]
    cos8 = jnp.concatenate([cos2] * (ATTN_WIDTH // LANES), axis=1)
    sin8 = jnp.concatenate([sin2] * (ATTN_WIDTH // LANES), axis=1)
    qr = (qn * cos8 + _swap_pairs(qn) * sin8) * (HEAD_DIM ** -0.5)
    for p in range(ATTN_WIDTH // LANES):
        q_ref[p] = qr[:, p * LANES:(p + 1) * LANES].astype(BF16)
    k0 = ATTN_WIDTH
    pk = _dot(h, w_ref[:, k0:k0 + KV_WIDTH])
    kn = pk * _head_rms(pk, gsum_ref, KV_WIDTH) * ks_ref[...]
    kr = kn * cos2 + _swap_pairs(kn) * sin2
    kt_ref[...] = kr.T.astype(BF16)
    v0 = k0 + KV_WIDTH
    pv = _dot(h, w_ref[:, v0:v0 + KV_WIDTH])
    vt_ref[...] = pv.T.astype(BF16)
    r0 = v0 + KV_WIDTH
    n_rest = r_ref.shape[1]
    step = 1024
    for j in range(n_rest // step):
        r_ref[:, j * step:(j + 1) * step] = _dot(
            h, w_ref[:, r0 + j * step:r0 + (j + 1) * step]).astype(BF16)


def _inproj(x2, scale1, shift1, w_in_b, cos2, sin2, qs, ks, gsum, B, S):
    T, D = x2.shape
    tm = min(TM_IN, S)
    npb = S // tm
    n_rest = w_in_b.shape[1] - ATTN_WIDTH - 2 * KV_WIDTH
    nq = ATTN_WIDTH // LANES
    const = lambda i: (0, 0)
    return pl.pallas_call(
        _inproj_kernel,
        grid=(T // tm,),
        in_specs=[
            pl.BlockSpec((tm, D), lambda i: (i, 0)),
            pl.BlockSpec((1, 1, D), lambda i: (i // npb, 0, 0)),
            pl.BlockSpec((1, 1, D), lambda i: (i // npb, 0, 0)),
            pl.BlockSpec(w_in_b.shape, const),
            pl.BlockSpec((tm, LANES), lambda i: (i % npb, 0)),
            pl.BlockSpec((tm, LANES), lambda i: (i % npb, 0)),
            pl.BlockSpec((1, ATTN_WIDTH), const),
            pl.BlockSpec((1, KV_WIDTH), const),
            pl.BlockSpec(gsum.shape, const),
        ],
        out_specs=[
            pl.BlockSpec((nq, tm, LANES), lambda i: (0, i, 0)),
            pl.BlockSpec((None, KV_WIDTH, tm), lambda i: (i // npb, 0, i % npb)),
            pl.BlockSpec((None, KV_WIDTH, tm), lambda i: (i // npb, 0, i % npb)),
            pl.BlockSpec((tm, n_rest), lambda i: (i, 0)),
        ],
        out_shape=[
            jax.ShapeDtypeStruct((nq, T, LANES), BF16),
            jax.ShapeDtypeStruct((B, KV_WIDTH, S), BF16),
            jax.ShapeDtypeStruct((B, KV_WIDTH, S), BF16),
            jax.ShapeDtypeStruct((T, n_rest), BF16),
        ],
        compiler_params=_params(1),
        name="inproj",
    )(x2, scale1, shift1, w_in_b, cos2, sin2, qs, ks, gsum)


def _attn_kernel(q_ref, kt_ref, vt_ref, o_ref):
    q2 = q_ref[...]
    kt = kt_ref[...]
    vt = vt_ref[...]
    outs = []
    for hh in range(LANES // HEAD_DIM):
        qh = q2[:, hh * HEAD_DIM:(hh + 1) * HEAD_DIM]
        s = _dot(qh, kt)
        m = jnp.max(s, axis=-1, keepdims=True)
        p = jnp.exp(s - m)
        l = jnp.sum(p, axis=-1, keepdims=True)
        o = _dot_nt(p.astype(BF16), vt)
        outs.append(o / l)
    o_ref[...] = jnp.concatenate(outs, axis=1).astype(BF16)


def _attention(q4, kt, vt, B, S):
    nq, T, _ = q4.shape
    tq = min(TQ, S)
    nqb = S // tq
    pairs_per_kv = nq // N_KV_HEADS
    return pl.pallas_call(
        _attn_kernel,
        grid=(B, N_KV_HEADS, nqb, pairs_per_kv),
        in_specs=[
            pl.BlockSpec((None, tq, LANES), lambda b, g, i, p: (g * pairs_per_kv + p, b * nqb + i, 0)),
            pl.BlockSpec((None, HEAD_DIM, S), lambda b, g, i, p: (b, g, 0)),
            pl.BlockSpec((None, HEAD_DIM, S), lambda b, g, i, p: (b, g, 0)),
        ],
        out_specs=pl.BlockSpec((tq, LANES), lambda b, g, i, p: (b * nqb + i, g * pairs_per_kv + p)),
        out_shape=jax.ShapeDtypeStruct((T, ATTN_WIDTH), BF16),
        compiler_params=_params(4),
        name="attention",
    )(q4, kt, vt)


def _gelu(a):
    return 0.5 * a * (1.0 + lax.erf(a * (2.0 ** -0.5)))


def _route(sel_t, scores_t):
    tm = sel_t.shape[1]
    neg = -jnp.inf
    row8 = lax.broadcasted_iota(I32, (GROUP_SIZE, tm), 0)
    groups = [sel_t[g * GROUP_SIZE:(g + 1) * GROUP_SIZE, :] for g in range(N_GROUPS)]
    gscore = []
    for a in groups:
        m1 = jnp.max(a, axis=0, keepdims=True)
        i1 = jnp.min(jnp.where(a == m1, row8, GROUP_SIZE), axis=0, keepdims=True)
        m2 = jnp.max(jnp.where(row8 == i1, neg, a), axis=0, keepdims=True)
        gscore.append(m1 + m2)
    masked = []
    for g in range(N_GROUPS):
        rank = jnp.zeros((1, tm), I32)
        for g2 in range(N_GROUPS):
            if g2 == g:
                continue
            if g2 < g:
                ahead = gscore[g2] >= gscore[g]
            else:
                ahead = gscore[g2] > gscore[g]
            rank = rank + ahead.astype(I32)
        keep = rank < TOPK_GROUPS
        masked.append(jnp.where(keep, groups[g], neg))
    cur = jnp.concatenate(masked, axis=0)
    row = lax.broadcasted_iota(I32, (N_EXPERTS, tm), 0)
    wts, hits = [], []
    chosen = jnp.zeros((N_EXPERTS, tm), F32)
    for _ in range(TOP_K):
        m = jnp.max(cur, axis=0, keepdims=True)
        idx = jnp.min(jnp.where(cur == m, row, N_EXPERTS), axis=0, keepdims=True)
        hit = row == idx
        wts.append(jnp.sum(jnp.where(hit, scores_t, 0.0), axis=0, keepdims=True))
        hits.append(hit)
        chosen = jnp.where(hit, 1.0, chosen)
        cur = jnp.where(hit, neg, cur)
    return wts, hits, chosen


def _mix_kernel(z_ref, ga_ref, gb_ref, oa_ref, x_ref, g1_ref, sc2_ref, sh2_ref,
                sg_ref, sb_ref, wcat_ref, bexp_ref, wa_ref, wb_ref, wo_ref,
                l1g_ref, l1b_ref, rwh_ref, rwl_ref, rb_ref, tri_ref, etri_ref,
                x1_ref, h2_ref, loc_ref, wts_ref, n8_ref, *, alpha):
    tm = z_ref.shape[0]
    z = _gelu(z_ref[...].astype(F32))
    u = z[:, :SGU_WIDTH]
    v = _layer_norm(z[:, SGU_WIDTH:], sg_ref[...], sb_ref[...])
    lane = lax.broadcasted_iota(I32, (SGU_CHUNK, SGU_WIDTH), 1)
    wcat = wcat_ref[...]
    svs = []
    for c in range(tm // SGU_CHUNK):
        vc = v[c * SGU_CHUNK:(c + 1) * SGU_CHUNK, :]
        stacked = jnp.concatenate(
            [jnp.where(lane // SGU_GROUP_DIM == g, vc, 0.0) for g in range(SGU_GROUPS)],
            axis=0).astype(BF16)
        svs.append(_dot(wcat, stacked) + bexp_ref[...])
    sv = jnp.concatenate(svs, axis=0) if len(svs) > 1 else svs[0]
    ob = (u * sv).astype(BF16)
    ya = _dot(oa_ref[...], wa_ref[...])
    yb = _dot(ob, wb_ref[...])
    y = (jax.nn.sigmoid(ga_ref[...].astype(F32)) * ya
         + jax.nn.sigmoid(gb_ref[...].astype(F32)) * yb)
    out = _dot(y.astype(BF16), wo_ref[...])
    x1 = _layer_norm(alpha * x_ref[...] + g1_ref[0] * out, l1g_ref[...], l1b_ref[...])
    x1_ref[...] = x1
    h2 = x1 * (1.0 + sc2_ref[0]) + sh2_ref[0]
    h2_ref[...] = h2.astype(BF16)
    hh, hl = _split_bf16(h2)
    rwh = rwh_ref[...]
    logits_t = _dot_nt(rwh, hh) + _dot_nt(rwh, hl) + _dot_nt(rwl_ref[...], hh)
    scores_t = jax.nn.sigmoid(logits_t)
    sel_t = scores_t + rb_ref[:, 0:1]
    wts, hits, chosen = _route(sel_t, scores_t)
    wsum = wts[0]
    for w in wts[1:]:
        wsum = wsum + w
    inv = ROUTED_SCALE / wsum
    rank = _dot(chosen.astype(BF16), tri_ref[...])
    n = jnp.sum(chosen, axis=1, keepdims=True)
    n8 = jnp.floor((n + (SUBLANES - 1)) * (1.0 / SUBLANES)) * SUBLANES
    n8_wide = jnp.broadcast_to(n8, (N_EXPERTS, LANES))
    run_off = _dot(etri_ref[...], n8_wide.astype(BF16))
    slot = rank + run_off[:, 0:1]
    locs = [jnp.sum(jnp.where(h, slot, 0.0), axis=0, keepdims=True) for h in hits]
    loc_ref[...] = jnp.concatenate(locs, axis=0).astype(I32)
    wts_ref[...] = jnp.concatenate([w * inv for w in wts], axis=0)
    n8_ref[...] = n8_wide.astype(I32)


def _mix(rest, oa, x2, gate1, scale2, shift2, sg, sb, wcat, bexp, wa, wb, wo, l1g, l1b,
         rwh, rwl, rb, tri, etri, B, S, alpha):
    T, D = x2.shape
    tm = min(TM_ROUTE, S)
    npb = S // tm
    const = lambda i: (0, 0)
    modspec = pl.BlockSpec((1, 1, D), lambda i: (i // npb, 0, 0))
    full = lambda a: pl.BlockSpec(a.shape, const)
    return pl.pallas_call(
        functools.partial(_mix_kernel, alpha=alpha),
        grid=(T // tm,),
        in_specs=[
            pl.BlockSpec((tm, D), lambda i: (i, 0)),
            pl.BlockSpec((tm, D), lambda i: (i, 1)),
            pl.BlockSpec((tm, D), lambda i: (i, 2)),
            pl.BlockSpec((tm, ATTN_WIDTH), lambda i: (i, 0)),
            pl.BlockSpec((tm, D), lambda i: (i, 0)),
            modspec, modspec, modspec,
            full(sg), full(sb), full(wcat), full(bexp), full(wa), full(wb), full(wo),
            full(l1g), full(l1b), full(rwh), full(rwl), full(rb), full(tri), full(etri),
        ],
        out_specs=[
            pl.BlockSpec((tm, D), lambda i: (i, 0)),
            pl.BlockSpec((tm, D), lambda i: (i, 0)),
            pl.BlockSpec((TOP_K, tm), lambda i: (0, i)),
            pl.BlockSpec((TOP_K, tm), lambda i: (0, i)),
            pl.BlockSpec((None, N_EXPERTS, LANES), lambda i: (i, 0, 0)),
        ],
        out_shape=[
            jax.ShapeDtypeStruct((T, D), F32),
            jax.ShapeDtypeStruct((T, D), BF16),
            jax.ShapeDtypeStruct((TOP_K, T), I32),
            jax.ShapeDtypeStruct((TOP_K, T), F32),
            jax.ShapeDtypeStruct((T // tm, N_EXPERTS, LANES), I32),
        ],
        compiler_params=_params(1),
        name="mix",
    )(rest, rest, rest, oa, x2, gate1, scale2, shift2, sg, sb, wcat, bexp, wa, wb, wo,
      l1g, l1b, rwh, rwl, rb, tri, etri)


SMEM_ROW = 128


def _smem_table(a):
    return jnp.pad(a.astype(I32), ((0, 0), (0, SMEM_ROW - a.shape[1]))).reshape(-1)


def _tile_rows(tm):
    return tm * TOP_K + N_EXPERTS * SUBLANES


def _run_copy(src_ref, s, dst_ref, d, n, sem):
    return pltpu.make_async_copy(src_ref.at[pl.ds(s, n)], dst_ref.at[pl.ds(d, n)], sem)


def _for_each_run(n_ref, fn):
    def body(e, c):
        n = pl.multiple_of(n_ref[e], SUBLANES)

        @pl.when(n > 0)
        def _():
            fn(e, n)
        return c
    lax.fori_loop(0, N_EXPERTS, body, 0)


def _dispatch_kernel(n8_ref, loff_ref, gbase_ref, fstart_ref, fn_ref, h_ref, loc_ref, zero_ref,
                     xs_ref, xp_ref, sem):
    i = pl.program_id(0)
    rows, tm = xp_ref.shape[0], h_ref.shape[0]

    def fill_copy(e, n):
        return _run_copy(zero_ref, 0, xs_ref, pl.multiple_of(fstart_ref[e], SUBLANES), n, sem)

    @pl.when(i == 0)
    def _():
        _for_each_run(fn_ref, lambda e, n: fill_copy(e, n).start())
        _for_each_run(fn_ref, lambda e, n: fill_copy(e, n).wait())

    loc = loc_ref[...]
    row = lax.broadcasted_iota(I32, (rows, tm), 0)
    perm = jnp.zeros((rows, tm), F32)
    for k in range(TOP_K):
        perm = jnp.where(row == loc[k:k + 1, :], 1.0, perm)
    xp_ref[...] = _pack_pairs(_dot(perm.astype(BF16), h_ref[...]))

    def run_copy(e, n):
        return _run_copy(xp_ref, pl.multiple_of(loff_ref[e], SUBLANES),
                         xs_ref, pl.multiple_of(gbase_ref[e], SUBLANES), n, sem)

    _for_each_run(n8_ref, lambda e, n: run_copy(e, n).start())
    _for_each_run(n8_ref, lambda e, n: run_copy(e, n).wait())


def _dispatch(n8_flat, loff_flat, gbase_flat, fill_start, fill_n, h2b, loc_t, n_rows):
    T, D = h2b.shape
    tm = min(TM_ROUTE, T)
    rows = _tile_rows(tm)
    zero_blk = jnp.zeros((BM_EXP, D // 2), U32)
    per_tile = pl.BlockSpec((SMEM_ROW,), lambda i: (i,), memory_space=pltpu.SMEM)
    once = pl.BlockSpec((N_EXPERTS,), lambda i: (0,), memory_space=pltpu.SMEM)
    return pl.pallas_call(
        _dispatch_kernel,
        grid=(T // tm,),
        in_specs=[
            per_tile, per_tile, per_tile, once, once,
            pl.BlockSpec((tm, D), lambda i: (i, 0)),
            pl.BlockSpec((TOP_K, tm), lambda i: (0, i)),
            pl.BlockSpec(zero_blk.shape, lambda i: (0, 0)),
        ],
        out_specs=pl.BlockSpec(memory_space=pl.ANY),
        out_shape=jax.ShapeDtypeStruct((n_rows, D // 2), U32),
        scratch_shapes=[pltpu.VMEM((rows, D // 2), U32), pltpu.SemaphoreType.DMA(())],
        compiler_params=_params(1),
        name="dispatch",
    )(n8_flat, loff_flat, gbase_flat, fill_start, fill_n, h2b, loc_t, zero_blk)


def _experts_kernel(be_ref, nused_ref, xs_ref, wg_ref, wu_ref, wd_ref, ys_ref,
                    wg_s, wu_s, wd_s):
    n = pl.program_id(0)

    @pl.when(n < nused_ref[0])
    def _():
        prev = be_ref[jnp.maximum(n - 1, 0)]
        first = jnp.logical_or(n == 0, be_ref[n] != prev)

        @pl.when(first)
        def _():
            wg_s[...] = wg_ref[...].astype(BF16)
            wu_s[...] = wu_ref[...].astype(BF16)
            wd_s[...] = wd_ref[...].astype(BF16)

        xb = _unpack_pairs(xs_ref[...])
        g = _dot(xb, wg_s[...])
        u = _dot(xb, wu_s[...])
        hb = (g * jax.nn.sigmoid(g) * u).astype(BF16)
        ys_ref[...] = _pack_pairs(_dot(hb, wd_s[...]))


def _experts(blk_expert, n_used, xs, w_gate, w_up, w_down, layer, n_blocks):
    _, E, D, FF = w_gate.shape
    bm = BM_EXP

    def row_map(n, be, nu):
        return (jnp.minimum(n, nu[0] - 1), 0)

    grid_spec = pltpu.PrefetchScalarGridSpec(
        num_scalar_prefetch=2,
        grid=(n_blocks,),
        in_specs=[
            pl.BlockSpec((bm, D // 2), row_map),
            pl.BlockSpec((None, None, D, FF), lambda n, be, nu: (layer, be[n], 0, 0)),
            pl.BlockSpec((None, None, D, FF), lambda n, be, nu: (layer, be[n], 0, 0)),
            pl.BlockSpec((None, None, FF, D), lambda n, be, nu: (layer, be[n], 0, 0)),
        ],
        out_specs=pl.BlockSpec((bm, D // 2), row_map),
        scratch_shapes=[
            pltpu.VMEM((D, FF), BF16),
            pltpu.VMEM((D, FF), BF16),
            pltpu.VMEM((FF, D), BF16),
        ],
    )
    return pl.pallas_call(
        _experts_kernel,
        grid_spec=grid_spec,
        out_shape=jax.ShapeDtypeStruct((n_blocks * bm, D // 2), U32),
        compiler_params=_params(1),
        name="experts",
    )(blk_expert, n_used, xs, w_gate, w_up, w_down)


def _combine_kernel(n8_ref, loff_ref, gbase_ref, ys_ref, loc_ref, w_ref, h_ref, x1_ref, g2_ref,
                    wsg_ref, wsu_ref, wsd_ref, l2g_ref, l2b_ref, o_ref, yp_ref, sem, *, alpha):
    i = pl.program_id(0)
    rows, tm = yp_ref.shape[0], h_ref.shape[0]

    @pl.when(i == 0)
    def _():
        yp_ref[...] = jnp.zeros_like(yp_ref)

    def run_copy(e, n):
        return _run_copy(ys_ref, pl.multiple_of(gbase_ref[e], SUBLANES),
                         yp_ref, pl.multiple_of(loff_ref[e], SUBLANES), n, sem)

    _for_each_run(n8_ref, lambda e, n: run_copy(e, n).start())

    hb = h_ref[...]
    g = _dot(hb, wsg_ref[...])
    u = _dot(hb, wsu_ref[...])
    y = _dot((g * jax.nn.sigmoid(g) * u).astype(BF16), wsd_ref[...])

    loc = loc_ref[...]
    w = w_ref[...]
    lane = lax.broadcasted_iota(I32, (tm, rows), 1)
    pw = jnp.zeros((tm, rows), F32)
    for k in range(TOP_K):
        pw = jnp.where(lane == loc[:, k:k + 1], w[:, k:k + 1], pw)
    pw_hi, pw_lo = _split_bf16(pw)

    _for_each_run(n8_ref, lambda e, n: run_copy(e, n).wait())

    yp = _unpack_pairs(yp_ref[...])
    y = y + _dot(pw_hi, yp) + _dot(pw_lo, yp)
    o_ref[...] = _layer_norm(alpha * x1_ref[...] + g2_ref[0] * y, l2g_ref[...], l2b_ref[...])


def _combine(n8_flat, loff_flat, gbase_flat, ys, loc_tk, w_tk, h2b, x1, gate2, wsg, wsu, wsd,
             l2g, l2b, B, S, alpha):
    T, D = h2b.shape
    tm = min(TM_ROUTE, S)
    npb = S // tm
    rows = _tile_rows(tm)
    const = lambda i: (0, 0)
    full = lambda a: pl.BlockSpec(a.shape, const)
    per_tile = pl.BlockSpec((SMEM_ROW,), lambda i: (i,), memory_space=pltpu.SMEM)
    return pl.pallas_call(
        functools.partial(_combine_kernel, alpha=alpha),
        grid=(T // tm,),
        in_specs=[
            per_tile, per_tile, per_tile,
            pl.BlockSpec(memory_space=pl.ANY),
            pl.BlockSpec((tm, TOP_K), lambda i: (i, 0)),
            pl.BlockSpec((tm, TOP_K), lambda i: (i, 0)),
            pl.BlockSpec((tm, D), lambda i: (i, 0)),
            pl.BlockSpec((tm, D), lambda i: (i, 0)),
            pl.BlockSpec((1, 1, D), lambda i: (i // npb, 0, 0)),
            full(wsg), full(wsu), full(wsd), full(l2g), full(l2b),
        ],
        out_specs=pl.BlockSpec((tm, D), lambda i: (i, 0)),
        out_shape=jax.ShapeDtypeStruct((T, D), F32),
        scratch_shapes=[pltpu.VMEM((rows, D // 2), U32), pltpu.SemaphoreType.DMA(())],
        compiler_params=_params(1),
        name="combine",
    )(n8_flat, loff_flat, gbase_flat, ys, loc_tk, w_tk, h2b, x1, gate2, wsg, wsu, wsd, l2g, l2b)


def _rope_tables(S):
    rows = S // GRID_W
    t = jnp.arange(S)
    row = (t // GRID_W - rows // 2).astype(F32)
    col = (t % GRID_W - GRID_W // 2).astype(F32)
    n_pairs = HEAD_DIM // 4
    inv_freq = ROPE_THETA ** (-jnp.arange(n_pairs, dtype=F32) / n_pairs)
    ang = jnp.concatenate([row[:, None] * inv_freq, col[:, None] * inv_freq], -1)
    cos, sin = jnp.cos(ang), jnp.sin(ang)
    cos_i = jnp.repeat(cos, 2, axis=-1)
    sin_i = jnp.stack([-sin, sin], axis=-1).reshape(S, HEAD_DIM)
    reps = LANES // HEAD_DIM
    return jnp.tile(cos_i, (1, reps)), jnp.tile(sin_i, (1, reps))


def kernel(x, c, w_ada, b_ada, w_in, q_scale, k_scale, sgu_ln_g, sgu_ln_b, w_spatial, b_spatial,
           w_branch_a, w_branch_b, w_out, ln1_g, ln1_b, router_w, router_bias, w_gate, w_up, w_down,
           ws_gate, ws_up, ws_down, ln2_g, ln2_b):
    B, S, D = x.shape
    T = B * S
    L = w_ada.shape[0]
    alpha = (2 * L) ** 0.25
    tm = min(TM_ROUTE, S)
    n_tiles = T // tm
    max_rows = T * TOP_K + n_tiles * N_EXPERTS * (SUBLANES - 1) + N_EXPERTS * (BM_EXP - 1)
    n_blocks = -(-max_rows // BM_EXP)
    n_rows = n_blocks * BM_EXP

    cos2, sin2 = _rope_tables(S)
    mod = _modulation(c, w_ada, b_ada)
    hd = jnp.arange(ATTN_WIDTH) // HEAD_DIM
    gsum = (hd[:, None] == hd[None, :]).astype(BF16)
    tri = (jnp.arange(tm)[:, None] < jnp.arange(tm)[None, :]).astype(BF16)
    etri = (jnp.arange(N_EXPERTS)[None, :] < jnp.arange(N_EXPERTS)[:, None]).astype(BF16)

    x2 = x.reshape(T, D)
    for l in range(L):
        m6 = mod[l].reshape(B, 1, 6, D)
        shift1, scale1, gate1, shift2, scale2, gate2 = [m6[:, :, j, :] for j in range(6)]
        qs = jnp.tile(q_scale[l], N_Q_HEADS).reshape(1, ATTN_WIDTH)
        ks = jnp.tile(k_scale[l], N_KV_HEADS).reshape(1, KV_WIDTH)
        q4, kt, vt, rest = _inproj(x2, scale1, shift1, w_in[l].astype(BF16), cos2, sin2, qs, ks,
                                   gsum, B, S)
        oa = _attention(q4, kt, vt, B, S)

        wcat = jnp.transpose(w_spatial[l], (1, 0, 2)).reshape(SGU_CHUNK, SGU_GROUPS * SGU_CHUNK)
        bexp = jnp.repeat(b_spatial[l].T, SGU_GROUP_DIM, axis=1)
        rw_t = router_w[l].T
        rwh = rw_t.astype(BF16)
        rwl = (rw_t - rwh.astype(F32)).astype(BF16)
        rb = jnp.broadcast_to(router_bias[l].astype(F32)[:, None], (N_EXPERTS, LANES))
        x1, h2b, loc_t, wts_t, n8_out = _mix(
            rest, oa, x2, gate1, scale2, shift2,
            sgu_ln_g[l].reshape(1, -1), sgu_ln_b[l].reshape(1, -1), wcat.astype(BF16), bexp,
            w_branch_a[l].astype(BF16), w_branch_b[l].astype(BF16), w_out[l].astype(BF16),
            ln1_g[l].reshape(1, -1), ln1_b[l].reshape(1, -1), rwh, rwl, rb, tri, etri, B, S, alpha)

        n8 = n8_out[:, :, 0]
        tot = jnp.sum(n8, axis=0)
        padded = (tot + BM_EXP - 1) // BM_EXP * BM_EXP
        pends = jnp.cumsum(padded)
        pstarts = pends - padded
        gbase = pstarts[None, :] + jnp.cumsum(n8, axis=0) - n8
        loff = jnp.cumsum(n8, axis=1) - n8
        fill_start = (pstarts + tot).astype(I32)
        fill_n = (padded - tot).astype(I32)
        n_used = (pends[-1] // BM_EXP).astype(I32).reshape(1)
        blk = jnp.minimum(jnp.arange(n_blocks, dtype=I32), n_used[0] - 1)
        blk_expert = jnp.minimum(
            jnp.sum((pends[None, :] <= (blk * BM_EXP)[:, None]).astype(I32), axis=1),
            N_EXPERTS - 1).astype(I32)
        n8_flat, loff_flat, gbase_flat = [_smem_table(a) for a in (n8, loff, gbase)]

        xs = _dispatch(n8_flat, loff_flat, gbase_flat, fill_start, fill_n, h2b, loc_t, n_rows)
        ys = _experts(blk_expert, n_used, xs, w_gate, w_up, w_down, l, n_blocks)
        x2 = _combine(n8_flat, loff_flat, gbase_flat, ys, loc_t.T, wts_t.T, h2b, x1, gate2,
                      ws_gate[l].astype(BF16), ws_up[l].astype(BF16), ws_down[l].astype(BF16),
                      ln2_g[l].reshape(1, -1), ln2_b[l].reshape(1, -1), B, S, alpha)
    return x2.reshape(B, S, D)
```

```python
import functools

import jax
import jax.numpy as jnp
from jax import lax
from jax.experimental import pallas as pl
from jax.experimental.pallas import tpu as pltpu

F32 = jnp.float32
BF16 = jnp.bfloat16
I32 = jnp.int32
U32 = jnp.uint32

GRID_W = 64
N_Q_HEADS = 8
N_KV_HEADS = 2
HEAD_DIM = 64
ROPE_THETA = 10000.0
ATTN_WIDTH = N_Q_HEADS * HEAD_DIM
KV_WIDTH = N_KV_HEADS * HEAD_DIM
SGU_GROUPS = 8
SGU_WIDTH = 512
SGU_GROUP_DIM = SGU_WIDTH // SGU_GROUPS
SGU_CHUNK = 128
N_EXPERTS = 64
TOP_K = 8
N_GROUPS = 8
TOPK_GROUPS = 4
GROUP_SIZE = N_EXPERTS // N_GROUPS
ROUTED_SCALE = 2.5
LN_EPS = 1e-5
RMS_EPS = 1e-6

LANES = 128
SUBLANES = 8
VMEM_LIMIT = 56 * 1024 * 1024

TM_IN = 256
TQ = 256
TM_ROUTE = 256
CH_EXP = 512


def _params(n_axes):
    return pltpu.CompilerParams(
        dimension_semantics=("arbitrary",) * n_axes, vmem_limit_bytes=VMEM_LIMIT)


def _dot(a, b):
    return jnp.dot(a, b, preferred_element_type=F32)


def _dot_nt(a, b):
    return lax.dot_general(a, b, (((1,), (1,)), ((), ())), preferred_element_type=F32)


def _split_bf16(a):
    hi = a.astype(BF16)
    lo = (a - hi.astype(F32)).astype(BF16)
    return hi, lo


def _layer_norm(r, g, b):
    mu = jnp.mean(r, axis=-1, keepdims=True)
    d = r - mu
    var = jnp.mean(d * d, axis=-1, keepdims=True)
    return d * lax.rsqrt(var + LN_EPS) * g + b


def _pack_pairs(y):
    n = y.shape[1] // 2
    hi = lax.bitcast_convert_type(y[:, :n].astype(BF16).astype(F32), U32)
    lo = lax.bitcast_convert_type(y[:, n:].astype(BF16).astype(F32), U32)
    return hi | (lo >> 16)


def _unpack_pairs(w):
    hi = lax.bitcast_convert_type(w & jnp.uint32(0xFFFF0000), F32).astype(BF16)
    lo = lax.bitcast_convert_type(w << 16, F32).astype(BF16)
    return jnp.concatenate([hi, lo], axis=1)


def _mod_kernel(c_ref, w_ref, b_ref, o_ref):
    c = c_ref[...]
    cond = c * jax.nn.sigmoid(c)
    o_ref[...] = _dot(cond, w_ref[...]) + b_ref[...]


def _modulation(c, w_ada, b_ada):
    L, D, D6 = w_ada.shape
    B = c.shape[0]
    nj = D6 // D
    return pl.pallas_call(
        _mod_kernel,
        grid=(L, nj),
        in_specs=[
            pl.BlockSpec((B, D), lambda l, j: (0, 0)),
            pl.BlockSpec((None, D, D), lambda l, j: (l, 0, j)),
            pl.BlockSpec((None, 1, D), lambda l, j: (l, 0, j)),
        ],
        out_specs=pl.BlockSpec((None, B, D), lambda l, j: (l, 0, j)),
        out_shape=jax.ShapeDtypeStruct((L, B, D6), F32),
        compiler_params=_params(2),
        name="modulation",
    )(c, w_ada, b_ada.reshape(L, 1, D6))


def _swap_pairs(a):
    n = a.shape[1]
    lane = lax.broadcasted_iota(I32, a.shape, 1)
    nxt = pltpu.roll(a, n - 1, axis=1)
    prv = pltpu.roll(a, 1, axis=1)
    return jnp.where((lane & 1) == 0, nxt, prv)


def _head_rms(p, gsum_ref, width):
    sq = p * p
    hi, lo = _split_bf16(sq)
    g = gsum_ref[0:width, 0:width]
    ss = _dot(hi, g) + _dot(lo, g)
    return lax.rsqrt(ss * (1.0 / HEAD_DIM) + RMS_EPS)


def _inproj_kernel(x_ref, sc_ref, sh_ref, w_ref, cos_ref, sin_ref, qs_ref, ks_ref, gsum_ref,
                   q_ref, kt_ref, vt_ref, r_ref):
    x = x_ref[...]
    h = (x * (1.0 + sc_ref[0]) + sh_ref[0]).astype(BF16)
    cos2 = cos_ref[...]
    sin2 = sin_ref[...]
    pq = _dot(h, w_ref[:, 0:ATTN_WIDTH])
    qn = pq * _head_rms(pq, gsum_ref, ATTN_WIDTH) * qs_ref[...]
    cos8 = jnp.concatenate([cos2] * (ATTN_WIDTH // LANES), axis=1)
    sin8 = jnp.concatenate([sin2] * (ATTN_WIDTH // LANES), axis=1)
    qr = (qn * cos8 + _swap_pairs(qn) * sin8) * (HEAD_DIM ** -0.5)
    for p in range(ATTN_WIDTH // LANES):
        q_ref[p] = qr[:, p * LANES:(p + 1) * LANES].astype(BF16)
    k0 = ATTN_WIDTH
    pk = _dot(h, w_ref[:, k0:k0 + KV_WIDTH])
    kn = pk * _head_rms(pk, gsum_ref, KV_WIDTH) * ks_ref[...]
    kr = kn * cos2 + _swap_pairs(kn) * sin2
    kt_ref[...] = kr.T.astype(BF16)
    v0 = k0 + KV_WIDTH
    pv = _dot(h, w_ref[:, v0:v0 + KV_WIDTH])
    vt_ref[...] = pv.T.astype(BF16)
    r0 = v0 + KV_WIDTH
    n_rest = r_ref.shape[1]
    step = 1024
    for j in range(n_rest // step):
        r_ref[:, j * step:(j + 1) * step] = _dot(
            h, w_ref[:, r0 + j * step:r0 + (j + 1) * step]).astype(BF16)


def _inproj(x2, scale1, shift1, w_in_b, cos2, sin2, qs, ks, gsum, B, S):
    T, D = x2.shape
    tm = min(TM_IN, S)
    npb = S // tm
    n_rest = w_in_b.shape[1] - ATTN_WIDTH - 2 * KV_WIDTH
    nq = ATTN_WIDTH // LANES
    const = lambda i: (0, 0)
    return pl.pallas_call(
        _inproj_kernel,
        grid=(T // tm,),
        in_specs=[
            pl.BlockSpec((tm, D), lambda i: (i, 0)),
            pl.BlockSpec((1, 1, D), lambda i: (i // npb, 0, 0)),
            pl.BlockSpec((1, 1, D), lambda i: (i // npb, 0, 0)),
            pl.BlockSpec(w_in_b.shape, const),
            pl.BlockSpec((tm, LANES), lambda i: (i % npb, 0)),
            pl.BlockSpec((tm, LANES), lambda i: (i % npb, 0)),
            pl.BlockSpec((1, ATTN_WIDTH), const),
            pl.BlockSpec((1, KV_WIDTH), const),
            pl.BlockSpec(gsum.shape, const),
        ],
        out_specs=[
            pl.BlockSpec((nq, tm, LANES), lambda i: (0, i, 0)),
            pl.BlockSpec((None, KV_WIDTH, tm), lambda i: (i // npb, 0, i % npb)),
            pl.BlockSpec((None, KV_WIDTH, tm), lambda i: (i // npb, 0, i % npb)),
            pl.BlockSpec((tm, n_rest), lambda i: (i, 0)),
        ],
        out_shape=[
            jax.ShapeDtypeStruct((nq, T, LANES), BF16),
            jax.ShapeDtypeStruct((B, KV_WIDTH, S), BF16),
            jax.ShapeDtypeStruct((B, KV_WIDTH, S), BF16),
            jax.ShapeDtypeStruct((T, n_rest), BF16),
        ],
        compiler_params=_params(1),
        name="inproj",
    )(x2, scale1, shift1, w_in_b, cos2, sin2, qs, ks, gsum)


def _attn_kernel(q_ref, kt_ref, vt_ref, o_ref):
    q2 = q_ref[...]
    kt = kt_ref[...]
    vt = vt_ref[...]
    outs = []
    for hh in range(LANES // HEAD_DIM):
        qh = q2[:, hh * HEAD_DIM:(hh + 1) * HEAD_DIM]
        s = _dot(qh, kt)
        m = jnp.max(s, axis=-1, keepdims=True)
        p = jnp.exp(s - m)
        l = jnp.sum(p, axis=-1, keepdims=True)
        o = _dot_nt(p.astype(BF16), vt)
        outs.append(o / l)
    o_ref[...] = jnp.concatenate(outs, axis=1).astype(BF16)


def _attention(q4, kt, vt, B, S):
    nq, T, _ = q4.shape
    tq = min(TQ, S)
    nqb = S // tq
    pairs_per_kv = nq // N_KV_HEADS
    return pl.pallas_call(
        _attn_kernel,
        grid=(B, N_KV_HEADS, nqb, pairs_per_kv),
        in_specs=[
            pl.BlockSpec((None, tq, LANES), lambda b, g, i, p: (g * pairs_per_kv + p, b * nqb + i, 0)),
            pl.BlockSpec((None, HEAD_DIM, S), lambda b, g, i, p: (b, g, 0)),
            pl.BlockSpec((None, HEAD_DIM, S), lambda b, g, i, p: (b, g, 0)),
        ],
        out_specs=pl.BlockSpec((tq, LANES), lambda b, g, i, p: (b * nqb + i, g * pairs_per_kv + p)),
        out_shape=jax.ShapeDtypeStruct((T, ATTN_WIDTH), BF16),
        compiler_params=_params(4),
        name="attention",
    )(q4, kt, vt)


def _gelu(a):
    return 0.5 * a * (1.0 + lax.erf(a * (2.0 ** -0.5)))


def _route(sel_t, scores_t):
    tm = sel_t.shape[1]
    neg = -jnp.inf
    row8 = lax.broadcasted_iota(I32, (GROUP_SIZE, tm), 0)
    groups = [sel_t[g * GROUP_SIZE:(g + 1) * GROUP_SIZE, :] for g in range(N_GROUPS)]
    gscore = []
    for a in groups:
        m1 = jnp.max(a, axis=0, keepdims=True)
        i1 = jnp.min(jnp.where(a == m1, row8, GROUP_SIZE), axis=0, keepdims=True)
        m2 = jnp.max(jnp.where(row8 == i1, neg, a), axis=0, keepdims=True)
        gscore.append(m1 + m2)
    masked = []
    for g in range(N_GROUPS):
        rank = jnp.zeros((1, tm), I32)
        for g2 in range(N_GROUPS):
            if g2 == g:
                continue
            if g2 < g:
                ahead = gscore[g2] >= gscore[g]
            else:
                ahead = gscore[g2] > gscore[g]
            rank = rank + ahead.astype(I32)
        keep = rank < TOPK_GROUPS
        masked.append(jnp.where(keep, groups[g], neg))
    cur = jnp.concatenate(masked, axis=0)
    row = lax.broadcasted_iota(I32, (N_EXPERTS, tm), 0)
    wts, hits = [], []
    chosen = jnp.zeros((N_EXPERTS, tm), F32)
    for _ in range(TOP_K):
        m = jnp.max(cur, axis=0, keepdims=True)
        idx = jnp.min(jnp.where(cur == m, row, N_EXPERTS), axis=0, keepdims=True)
        hit = row == idx
        wts.append(jnp.sum(jnp.where(hit, scores_t, 0.0), axis=0, keepdims=True))
        hits.append(hit)
        chosen = jnp.where(hit, 1.0, chosen)
        cur = jnp.where(hit, neg, cur)
    return wts, hits, chosen


def _mix_kernel(z_ref, ga_ref, gb_ref, oa_ref, x_ref, g1_ref, sc2_ref, sh2_ref,
                sg_ref, sb_ref, wcat_ref, bexp_ref, wa_ref, wb_ref, wo_ref,
                l1g_ref, l1b_ref, rwh_ref, rwl_ref, rb_ref, tri_ref, etri_ref,
                x1_ref, h2_ref, loc_ref, wts_ref, n8_ref, *, alpha):
    tm = z_ref.shape[0]
    z = _gelu(z_ref[...].astype(F32))
    u = z[:, :SGU_WIDTH]
    v = _layer_norm(z[:, SGU_WIDTH:], sg_ref[...], sb_ref[...])
    lane = lax.broadcasted_iota(I32, (SGU_CHUNK, SGU_WIDTH), 1)
    wcat = wcat_ref[...]
    svs = []
    for c in range(tm // SGU_CHUNK):
        vc = v[c * SGU_CHUNK:(c + 1) * SGU_CHUNK, :]
        stacked = jnp.concatenate(
            [jnp.where(lane // SGU_GROUP_DIM == g, vc, 0.0) for g in range(SGU_GROUPS)],
            axis=0).astype(BF16)
        svs.append(_dot(wcat, stacked) + bexp_ref[...])
    sv = jnp.concatenate(svs, axis=0) if len(svs) > 1 else svs[0]
    ob = (u * sv).astype(BF16)
    ya = _dot(oa_ref[...], wa_ref[...])
    yb = _dot(ob, wb_ref[...])
    y = (jax.nn.sigmoid(ga_ref[...].astype(F32)) * ya
         + jax.nn.sigmoid(gb_ref[...].astype(F32)) * yb)
    out = _dot(y.astype(BF16), wo_ref[...])
    x1 = _layer_norm(alpha * x_ref[...] + g1_ref[0] * out, l1g_ref[...], l1b_ref[...])
    x1_ref[...] = x1
    h2 = x1 * (1.0 + sc2_ref[0]) + sh2_ref[0]
    h2_ref[...] = h2.astype(BF16)
    hh, hl = _split_bf16(h2)
    rwh = rwh_ref[...]
    logits_t = _dot_nt(rwh, hh) + _dot_nt(rwh, hl) + _dot_nt(rwl_ref[...], hh)
    scores_t = jax.nn.sigmoid(logits_t)
    sel_t = scores_t + rb_ref[:, 0:1]
    wts, hits, chosen = _route(sel_t, scores_t)
    wsum = wts[0]
    for w in wts[1:]:
        wsum = wsum + w
    inv = ROUTED_SCALE / wsum
    rank = _dot(chosen.astype(BF16), tri_ref[...])
    n = jnp.sum(chosen, axis=1, keepdims=True)
    n8 = jnp.floor((n + (SUBLANES - 1)) * (1.0 / SUBLANES)) * SUBLANES
    n8_wide = jnp.broadcast_to(n8, (N_EXPERTS, LANES))
    run_off = _dot(etri_ref[...], n8_wide.astype(BF16))
    slot = rank + run_off[:, 0:1]
    locs = [jnp.sum(jnp.where(h, slot, 0.0), axis=0, keepdims=True) for h in hits]
    loc_ref[...] = jnp.concatenate(locs, axis=0).astype(I32)
    wts_ref[...] = jnp.concatenate([w * inv for w in wts], axis=0)
    n8_ref[...] = n8_wide.astype(I32)


def _mix(rest, oa, x2, gate1, scale2, shift2, sg, sb, wcat, bexp, wa, wb, wo, l1g, l1b,
         rwh, rwl, rb, tri, etri, B, S, alpha):
    T, D = x2.shape
    tm = min(TM_ROUTE, S)
    npb = S // tm
    const = lambda i: (0, 0)
    modspec = pl.BlockSpec((1, 1, D), lambda i: (i // npb, 0, 0))
    full = lambda a: pl.BlockSpec(a.shape, const)
    return pl.pallas_call(
        functools.partial(_mix_kernel, alpha=alpha),
        grid=(T // tm,),
        in_specs=[
            pl.BlockSpec((tm, D), lambda i: (i, 0)),
            pl.BlockSpec((tm, D), lambda i: (i, 1)),
            pl.BlockSpec((tm, D), lambda i: (i, 2)),
            pl.BlockSpec((tm, ATTN_WIDTH), lambda i: (i, 0)),
            pl.BlockSpec((tm, D), lambda i: (i, 0)),
            modspec, modspec, modspec,
            full(sg), full(sb), full(wcat), full(bexp), full(wa), full(wb), full(wo),
            full(l1g), full(l1b), full(rwh), full(rwl), full(rb), full(tri), full(etri),
        ],
        out_specs=[
            pl.BlockSpec((tm, D), lambda i: (i, 0)),
            pl.BlockSpec((tm, D), lambda i: (i, 0)),
            pl.BlockSpec((TOP_K, tm), lambda i: (0, i)),
            pl.BlockSpec((TOP_K, tm), lambda i: (0, i)),
            pl.BlockSpec((None, N_EXPERTS, LANES), lambda i: (i, 0, 0)),
        ],
        out_shape=[
            jax.ShapeDtypeStruct((T, D), F32),
            jax.ShapeDtypeStruct((T, D), BF16),
            jax.ShapeDtypeStruct((TOP_K, T), I32),
            jax.ShapeDtypeStruct((TOP_K, T), F32),
            jax.ShapeDtypeStruct((T // tm, N_EXPERTS, LANES), I32),
        ],
        compiler_params=_params(1),
        name="mix",
    )(rest, rest, rest, oa, x2, gate1, scale2, shift2, sg, sb, wcat, bexp, wa, wb, wo,
      l1g, l1b, rwh, rwl, rb, tri, etri)


SMEM_ROW = 128


def _smem_table(a):
    return jnp.pad(a.astype(I32), ((0, 0), (0, SMEM_ROW - a.shape[1]))).reshape(-1)


def _tile_rows(tm):
    return tm * TOP_K + N_EXPERTS * SUBLANES


def _run_copy(src_ref, s, dst_ref, d, n, sem):
    return pltpu.make_async_copy(src_ref.at[pl.ds(s, n)], dst_ref.at[pl.ds(d, n)], sem)


def _for_each_run(n_ref, fn):
    def body(e, c):
        n = pl.multiple_of(n_ref[e], SUBLANES)

        @pl.when(n > 0)
        def _():
            fn(e, n)
        return c
    lax.fori_loop(0, N_EXPERTS, body, 0)


def _dispatch_kernel(n8_ref, loff_ref, gbase_ref, n8_prev_ref, h_ref, loc_ref,
                     xs_ref, xp_ref, sems):
    i = pl.program_id(0)
    rows, tm = xp_ref.shape[1], h_ref.shape[0]
    slot = i % 2

    loc = loc_ref[...]
    row = lax.broadcasted_iota(I32, (rows, tm), 0)
    perm = jnp.zeros((rows, tm), F32)
    for k in range(TOP_K):
        perm = jnp.where(row == loc[k:k + 1, :], 1.0, perm)
    xp_ref[slot] = _pack_pairs(_dot(perm.astype(BF16), h_ref[...]))

    def run_copy(e, n):
        return _run_copy(xp_ref.at[slot], pl.multiple_of(loff_ref[e], SUBLANES),
                         xs_ref, pl.multiple_of(gbase_ref[e], SUBLANES), n, sems.at[slot])

    def run_wait(s):
        return lambda e, n: _run_copy(xp_ref.at[s], 0, xs_ref, 0, n, sems.at[s]).wait()

    _for_each_run(n8_ref, lambda e, n: run_copy(e, n).start())

    @pl.when(i > 0)
    def _():
        _for_each_run(n8_prev_ref, run_wait(1 - slot))

    @pl.when(i == pl.num_programs(0) - 1)
    def _():
        _for_each_run(n8_ref, run_wait(slot))


def _dispatch(n8_flat, loff_flat, gbase_flat, h2b, loc_t, n_rows):
    T, D = h2b.shape
    tm = min(TM_ROUTE, T)
    rows = _tile_rows(tm)
    per_tile = pl.BlockSpec((SMEM_ROW,), lambda i: (i,), memory_space=pltpu.SMEM)
    prev_tile = pl.BlockSpec((SMEM_ROW,), lambda i: (jnp.maximum(i - 1, 0),),
                             memory_space=pltpu.SMEM)
    return pl.pallas_call(
        _dispatch_kernel,
        grid=(T // tm,),
        in_specs=[
            per_tile, per_tile, per_tile, prev_tile,
            pl.BlockSpec((tm, D), lambda i: (i, 0)),
            pl.BlockSpec((TOP_K, tm), lambda i: (0, i)),
        ],
        out_specs=pl.BlockSpec(memory_space=pl.ANY),
        out_shape=jax.ShapeDtypeStruct((n_rows, D // 2), U32),
        scratch_shapes=[pltpu.VMEM((2, rows, D // 2), U32), pltpu.SemaphoreType.DMA((2,))],
        compiler_params=_params(1),
        name="dispatch",
    )(n8_flat, loff_flat, gbase_flat, n8_flat, h2b, loc_t)


N_XBUF = 3
N_YBUF = 2


def _experts_kernel(gstart_ref, nch_ref, cstart_ref, crows_ref, total_ref,
                    xs_ref, wg_ref, wu_ref, wd_ref, ys_ref,
                    xbuf, ybuf, wg_s, wu_s, wd_s, sem_in, sem_out):
    e = pl.program_id(0)
    total = total_ref[0]

    def in_copy(g):
        s = g % N_XBUF
        n = pl.multiple_of(crows_ref[g], SUBLANES)
        return _run_copy(xs_ref, pl.multiple_of(cstart_ref[g], SUBLANES), xbuf.at[s], 0, n,
                         sem_in.at[s])

    def out_copy(g):
        s = g % N_YBUF
        n = pl.multiple_of(crows_ref[g], SUBLANES)
        return _run_copy(ybuf.at[s], 0, ys_ref, pl.multiple_of(cstart_ref[g], SUBLANES), n,
                         sem_out.at[s])

    def when_chunk(g, fn):
        @pl.when(jnp.logical_and(g >= 0, g < total))
        def _():
            fn(g)

    @pl.when(e == 0)
    def _():
        xbuf[...] = jnp.zeros_like(xbuf)
        for g0 in range(N_XBUF - 1):
            when_chunk(g0, lambda g: in_copy(g).start())

    wg_s[...] = wg_ref[...].astype(BF16)
    wu_s[...] = wu_ref[...].astype(BF16)
    wd_s[...] = wd_ref[...].astype(BF16)

    def chunk(c, carry):
        g = gstart_ref[e] + c
        in_copy(g).wait()
        when_chunk(g + (N_XBUF - 1), lambda g2: in_copy(g2).start())
        xb = _unpack_pairs(xbuf[g % N_XBUF])
        gate = _dot(xb, wg_s[...])
        up = _dot(xb, wu_s[...])
        hb = (gate * jax.nn.sigmoid(gate) * up).astype(BF16)
        y = _pack_pairs(_dot(hb, wd_s[...]))
        when_chunk(g - N_YBUF, lambda g2: out_copy(g2).wait())
        ybuf[g % N_YBUF] = y
        out_copy(g).start()
        return carry
    lax.fori_loop(0, nch_ref[e], chunk, 0)

    @pl.when(e == pl.num_programs(0) - 1)
    def _():
        for back in range(N_YBUF, 0, -1):
            when_chunk(total - back, lambda g: out_copy(g).wait())


def _experts(gstart, nch, cstart, crows, total, xs, w_gate, w_up, w_down, layer):
    _, E, D, FF = w_gate.shape
    wspec = lambda a, b: pl.BlockSpec((None, None, a, b), lambda e, *_: (layer, e, 0, 0))
    grid_spec = pltpu.PrefetchScalarGridSpec(
        num_scalar_prefetch=5,
        grid=(E,),
        in_specs=[pl.BlockSpec(memory_space=pl.ANY), wspec(D, FF), wspec(D, FF), wspec(FF, D)],
        out_specs=pl.BlockSpec(memory_space=pl.ANY),
        scratch_shapes=[
            pltpu.VMEM((N_XBUF, CH_EXP, D // 2), U32),
            pltpu.VMEM((N_YBUF, CH_EXP, D // 2), U32),
            pltpu.VMEM((D, FF), BF16),
            pltpu.VMEM((D, FF), BF16),
            pltpu.VMEM((FF, D), BF16),
            pltpu.SemaphoreType.DMA((N_XBUF,)),
            pltpu.SemaphoreType.DMA((N_YBUF,)),
        ],
    )
    return pl.pallas_call(
        _experts_kernel,
        grid_spec=grid_spec,
        out_shape=jax.ShapeDtypeStruct(xs.shape, U32),
        compiler_params=_params(1),
        name="experts",
    )(gstart, nch, cstart, crows, total, xs, w_gate, w_up, w_down)


def _combine_kernel(n8_ref, loff_ref, gbase_ref, ys_ref, loc_ref, w_ref, h_ref, x1_ref, g2_ref,
                    wsg_ref, wsu_ref, wsd_ref, l2g_ref, l2b_ref, o_ref, yp_ref, sem, *, alpha):
    i = pl.program_id(0)
    rows, tm = yp_ref.shape[0], h_ref.shape[0]

    @pl.when(i == 0)
    def _():
        yp_ref[...] = jnp.zeros_like(yp_ref)

    def run_copy(e, n):
        return _run_copy(ys_ref, pl.multiple_of(gbase_ref[e], SUBLANES),
                         yp_ref, pl.multiple_of(loff_ref[e], SUBLANES), n, sem)

    _for_each_run(n8_ref, lambda e, n: run_copy(e, n).start())

    hb = h_ref[...]
    g = _dot(hb, wsg_ref[...])
    u = _dot(hb, wsu_ref[...])
    y = _dot((g * jax.nn.sigmoid(g) * u).astype(BF16), wsd_ref[...])

    loc = loc_ref[...]
    w = w_ref[...]
    lane = lax.broadcasted_iota(I32, (tm, rows), 1)
    pw = jnp.zeros((tm, rows), F32)
    for k in range(TOP_K):
        pw = jnp.where(lane == loc[:, k:k + 1], w[:, k:k + 1], pw)
    pw = pw.astype(BF16)

    _for_each_run(n8_ref, lambda e, n: run_copy(e, n).wait())

    yp = _unpack_pairs(yp_ref[...])
    y = y + _dot(pw, yp)
    o_ref[...] = _layer_norm(alpha * x1_ref[...] + g2_ref[0] * y, l2g_ref[...], l2b_ref[...])


def _combine(n8_flat, loff_flat, gbase_flat, ys, loc_tk, w_tk, h2b, x1, gate2, wsg, wsu, wsd,
             l2g, l2b, B, S, alpha):
    T, D = h2b.shape
    tm = min(TM_ROUTE, S)
    npb = S // tm
    rows = _tile_rows(tm)
    const = lambda i: (0, 0)
    full = lambda a: pl.BlockSpec(a.shape, const)
    per_tile = pl.BlockSpec((SMEM_ROW,), lambda i: (i,), memory_space=pltpu.SMEM)
    return pl.pallas_call(
        functools.partial(_combine_kernel, alpha=alpha),
        grid=(T // tm,),
        in_specs=[
            per_tile, per_tile, per_tile,
            pl.BlockSpec(memory_space=pl.ANY),
            pl.BlockSpec((tm, TOP_K), lambda i: (i, 0)),
            pl.BlockSpec((tm, TOP_K), lambda i: (i, 0)),
            pl.BlockSpec((tm, D), lambda i: (i, 0)),
            pl.BlockSpec((tm, D), lambda i: (i, 0)),
            pl.BlockSpec((1, 1, D), lambda i: (i // npb, 0, 0)),
            full(wsg), full(wsu), full(wsd), full(l2g), full(l2b),
        ],
        out_specs=pl.BlockSpec((tm, D), lambda i: (i, 0)),
        out_shape=jax.ShapeDtypeStruct((T, D), F32),
        scratch_shapes=[pltpu.VMEM((rows, D // 2), U32), pltpu.SemaphoreType.DMA(())],
        compiler_params=_params(1),
        name="combine",
    )(n8_flat, loff_flat, gbase_flat, ys, loc_tk, w_tk, h2b, x1, gate2, wsg, wsu, wsd, l2g, l2b)


def _rope_tables(S):
    rows = S // GRID_W
    t = jnp.arange(S)
    row = (t // GRID_W - rows // 2).astype(F32)
    col = (t % GRID_W - GRID_W // 2).astype(F32)
    n_pairs = HEAD_DIM // 4
    inv_freq = ROPE_THETA ** (-jnp.arange(n_pairs, dtype=F32) / n_pairs)
    ang = jnp.concatenate([row[:, None] * inv_freq, col[:, None] * inv_freq], -1)
    cos, sin = jnp.cos(ang), jnp.sin(ang)
    cos_i = jnp.repeat(cos, 2, axis=-1)
    sin_i = jnp.stack([-sin, sin], axis=-1).reshape(S, HEAD_DIM)
    reps = LANES // HEAD_DIM
    return jnp.tile(cos_i, (1, reps)), jnp.tile(sin_i, (1, reps))


def kernel(x, c, w_ada, b_ada, w_in, q_scale, k_scale, sgu_ln_g, sgu_ln_b, w_spatial, b_spatial,
           w_branch_a, w_branch_b, w_out, ln1_g, ln1_b, router_w, router_bias, w_gate, w_up, w_down,
           ws_gate, ws_up, ws_down, ln2_g, ln2_b):
    B, S, D = x.shape
    T = B * S
    L = w_ada.shape[0]
    alpha = (2 * L) ** 0.25
    tm = min(TM_ROUTE, S)
    n_tiles = T // tm
    n_rows = T * TOP_K + n_tiles * N_EXPERTS * SUBLANES
    max_chunks = n_rows // CH_EXP + N_EXPERTS

    cos2, sin2 = _rope_tables(S)
    mod = _modulation(c, w_ada, b_ada)
    hd = jnp.arange(ATTN_WIDTH) // HEAD_DIM
    gsum = (hd[:, None] == hd[None, :]).astype(BF16)
    tri = (jnp.arange(tm)[:, None] < jnp.arange(tm)[None, :]).astype(BF16)
    etri = (jnp.arange(N_EXPERTS)[None, :] < jnp.arange(N_EXPERTS)[:, None]).astype(BF16)

    x2 = x.reshape(T, D)
    for l in range(L):
        m6 = mod[l].reshape(B, 1, 6, D)
        shift1, scale1, gate1, shift2, scale2, gate2 = [m6[:, :, j, :] for j in range(6)]
        qs = jnp.tile(q_scale[l], N_Q_HEADS).reshape(1, ATTN_WIDTH)
        ks = jnp.tile(k_scale[l], N_KV_HEADS).reshape(1, KV_WIDTH)
        q4, kt, vt, rest = _inproj(x2, scale1, shift1, w_in[l].astype(BF16), cos2, sin2, qs, ks,
                                   gsum, B, S)
        oa = _attention(q4, kt, vt, B, S)

        wcat = jnp.transpose(w_spatial[l], (1, 0, 2)).reshape(SGU_CHUNK, SGU_GROUPS * SGU_CHUNK)
        bexp = jnp.repeat(b_spatial[l].T, SGU_GROUP_DIM, axis=1)
        rw_t = router_w[l].T
        rwh = rw_t.astype(BF16)
        rwl = (rw_t - rwh.astype(F32)).astype(BF16)
        rb = jnp.broadcast_to(router_bias[l].astype(F32)[:, None], (N_EXPERTS, LANES))
        x1, h2b, loc_t, wts_t, n8_out = _mix(
            rest, oa, x2, gate1, scale2, shift2,
            sgu_ln_g[l].reshape(1, -1), sgu_ln_b[l].reshape(1, -1), wcat.astype(BF16), bexp,
            w_branch_a[l].astype(BF16), w_branch_b[l].astype(BF16), w_out[l].astype(BF16),
            ln1_g[l].reshape(1, -1), ln1_b[l].reshape(1, -1), rwh, rwl, rb, tri, etri, B, S, alpha)

        n8 = n8_out[:, :, 0]
        tot = jnp.sum(n8, axis=0)
        pstarts = jnp.cumsum(tot) - tot
        gbase = pstarts[None, :] + jnp.cumsum(n8, axis=0) - n8
        loff = jnp.cumsum(n8, axis=1) - n8
        n8_flat, loff_flat, gbase_flat = [_smem_table(a) for a in (n8, loff, gbase)]
        nch = (tot + CH_EXP - 1) // CH_EXP
        gend = jnp.cumsum(nch)
        gstart = gend - nch
        gidx = jnp.arange(max_chunks, dtype=I32)
        owner = (gend[None, :] <= gidx[:, None]).astype(I32).sum(axis=1)
        onehot = (owner[:, None] == jnp.arange(N_EXPERTS)[None, :]).astype(I32)
        pick = lambda v: jnp.sum(onehot * v[None, :], axis=1)
        within = (gidx - pick(gstart)) * CH_EXP
        crows = jnp.clip(pick(tot) - within, 0, CH_EXP).astype(I32)
        cstart = (pick(pstarts) + within).astype(I32)
        total = gend[-1].astype(I32).reshape(1)

        xs = _dispatch(n8_flat, loff_flat, gbase_flat, h2b, loc_t, n_rows)
        ys = _experts(gstart.astype(I32), nch.astype(I32), cstart, crows, total, xs,
                      w_gate, w_up, w_down, l)
        x2 = _combine(n8_flat, loff_flat, gbase_flat, ys, loc_t.T, wts_t.T, h2b, x1, gate2,
                      ws_gate[l].astype(BF16), ws_up[l].astype(BF16), ws_down[l].astype(BF16),
                      ln2_g[l].reshape(1, -1), ln2_b[l].reshape(1, -1), B, S, alpha)
    return x2.reshape(B, S, D)
```

```python
import functools

import jax
import jax.numpy as jnp
from jax import lax
from jax.experimental import pallas as pl
from jax.experimental.pallas import tpu as pltpu

F32 = jnp.float32
BF16 = jnp.bfloat16
I32 = jnp.int32
U32 = jnp.uint32

GRID_W = 64
N_Q_HEADS = 8
N_KV_HEADS = 2
HEAD_DIM = 64
ROPE_THETA = 10000.0
ATTN_WIDTH = N_Q_HEADS * HEAD_DIM
KV_WIDTH = N_KV_HEADS * HEAD_DIM
SGU_GROUPS = 8
SGU_WIDTH = 512
SGU_GROUP_DIM = SGU_WIDTH // SGU_GROUPS
SGU_CHUNK = 128
N_EXPERTS = 64
TOP_K = 8
N_GROUPS = 8
TOPK_GROUPS = 4
GROUP_SIZE = N_EXPERTS // N_GROUPS
ROUTED_SCALE = 2.5
LN_EPS = 1e-5
RMS_EPS = 1e-6

LANES = 128
SUBLANES = 8
VMEM_LIMIT = 56 * 1024 * 1024

TM_IN = 256
TQ = 256
KV_CHUNK = 512
LOG2_E = 1.4426950408889634
TM_ROUTE = 256
CH_EXP = 512


def _params(n_axes):
    return pltpu.CompilerParams(
        dimension_semantics=("arbitrary",) * n_axes, vmem_limit_bytes=VMEM_LIMIT)


def _dot(a, b):
    return jnp.dot(a, b, preferred_element_type=F32)


def _dot_nt(a, b):
    return lax.dot_general(a, b, (((1,), (1,)), ((), ())), preferred_element_type=F32)


def _split_bf16(a):
    hi = a.astype(BF16)
    lo = (a - hi.astype(F32)).astype(BF16)
    return hi, lo


def _layer_norm(r, g, b):
    mu = jnp.mean(r, axis=-1, keepdims=True)
    d = r - mu
    var = jnp.mean(d * d, axis=-1, keepdims=True)
    return d * lax.rsqrt(var + LN_EPS) * g + b


def _pack_pairs(y):
    n = y.shape[1] // 2
    hi = lax.bitcast_convert_type(y[:, :n].astype(BF16).astype(F32), U32)
    lo = lax.bitcast_convert_type(y[:, n:].astype(BF16).astype(F32), U32)
    return hi | (lo >> 16)


def _unpack_pairs(w):
    hi = lax.bitcast_convert_type(w & jnp.uint32(0xFFFF0000), F32).astype(BF16)
    lo = lax.bitcast_convert_type(w << 16, F32).astype(BF16)
    return jnp.concatenate([hi, lo], axis=1)


def _mod_kernel(c_ref, w_ref, b_ref, o_ref):
    c = c_ref[...]
    cond = c * jax.nn.sigmoid(c)
    o_ref[...] = _dot(cond, w_ref[...]) + b_ref[...]


def _modulation(c, w_ada, b_ada):
    L, D, D6 = w_ada.shape
    B = c.shape[0]
    nj = D6 // D
    return pl.pallas_call(
        _mod_kernel,
        grid=(L, nj),
        in_specs=[
            pl.BlockSpec((B, D), lambda l, j: (0, 0)),
            pl.BlockSpec((None, D, D), lambda l, j: (l, 0, j)),
            pl.BlockSpec((None, 1, D), lambda l, j: (l, 0, j)),
        ],
        out_specs=pl.BlockSpec((None, B, D), lambda l, j: (l, 0, j)),
        out_shape=jax.ShapeDtypeStruct((L, B, D6), F32),
        compiler_params=_params(2),
        name="modulation",
    )(c, w_ada, b_ada.reshape(L, 1, D6))


def _swap_pairs(a):
    n = a.shape[1]
    lane = lax.broadcasted_iota(I32, a.shape, 1)
    nxt = pltpu.roll(a, n - 1, axis=1)
    prv = pltpu.roll(a, 1, axis=1)
    return jnp.where((lane & 1) == 0, nxt, prv)


def _head_rms(p, gsum_ref, width):
    sq = p * p
    hi, lo = _split_bf16(sq)
    g = gsum_ref[0:width, 0:width]
    ss = _dot(hi, g) + _dot(lo, g)
    return lax.rsqrt(ss * (1.0 / HEAD_DIM) + RMS_EPS)


def _inproj_kernel(x_ref, sc_ref, sh_ref, w_ref, cos_ref, sin_ref, qs_ref, ks_ref, gsum_ref,
                   q_ref, kt_ref, vt_ref, r_ref):
    x = x_ref[...]
    h = (x * (1.0 + sc_ref[0]) + sh_ref[0]).astype(BF16)
    cos2 = cos_ref[...]
    sin2 = sin_ref[...]
    pq = _dot(h, w_ref[:, 0:ATTN_WIDTH])
    qn = pq * _head_rms(pq, gsum_ref, ATTN_WIDTH) * qs_ref[...]
    cos8 = jnp.concatenate([cos2] * (ATTN_WIDTH // LANES), axis=1)
    sin8 = jnp.concatenate([sin2] * (ATTN_WIDTH // LANES), axis=1)
    qr = (qn * cos8 + _swap_pairs(qn) * sin8) * (HEAD_DIM ** -0.5 * LOG2_E)
    for p in range(ATTN_WIDTH // LANES):
        q_ref[p] = qr[:, p * LANES:(p + 1) * LANES].astype(BF16)
    k0 = ATTN_WIDTH
    pk = _dot(h, w_ref[:, k0:k0 + KV_WIDTH])
    kn = pk * _head_rms(pk, gsum_ref, KV_WIDTH) * ks_ref[...]
    kr = kn * cos2 + _swap_pairs(kn) * sin2
    kt_ref[...] = kr.T.astype(BF16)
    v0 = k0 + KV_WIDTH
    pv = _dot(h, w_ref[:, v0:v0 + KV_WIDTH])
    vt_ref[...] = pv.T.astype(BF16)
    r0 = v0 + KV_WIDTH
    n_rest = r_ref.shape[1]
    step = 1024
    for j in range(n_rest // step):
        r_ref[:, j * step:(j + 1) * step] = _dot(
            h, w_ref[:, r0 + j * step:r0 + (j + 1) * step]).astype(BF16)


def _inproj(x2, scale1, shift1, w_in_b, cos2, sin2, qs, ks, gsum, B, S):
    T, D = x2.shape
    tm = min(TM_IN, S)
    npb = S // tm
    n_rest = w_in_b.shape[1] - ATTN_WIDTH - 2 * KV_WIDTH
    nq = ATTN_WIDTH // LANES
    const = lambda i: (0, 0)
    return pl.pallas_call(
        _inproj_kernel,
        grid=(T // tm,),
        in_specs=[
            pl.BlockSpec((tm, D), lambda i: (i, 0)),
            pl.BlockSpec((1, 1, D), lambda i: (i // npb, 0, 0)),
            pl.BlockSpec((1, 1, D), lambda i: (i // npb, 0, 0)),
            pl.BlockSpec(w_in_b.shape, const),
            pl.BlockSpec((tm, LANES), lambda i: (i % npb, 0)),
            pl.BlockSpec((tm, LANES), lambda i: (i % npb, 0)),
            pl.BlockSpec((1, ATTN_WIDTH), const),
            pl.BlockSpec((1, KV_WIDTH), const),
            pl.BlockSpec(gsum.shape, const),
        ],
        out_specs=[
            pl.BlockSpec((nq, tm, LANES), lambda i: (0, i, 0)),
            pl.BlockSpec((None, KV_WIDTH, tm), lambda i: (i // npb, 0, i % npb)),
            pl.BlockSpec((None, KV_WIDTH, tm), lambda i: (i // npb, 0, i % npb)),
            pl.BlockSpec((tm, n_rest), lambda i: (i, 0)),
        ],
        out_shape=[
            jax.ShapeDtypeStruct((nq, T, LANES), BF16),
            jax.ShapeDtypeStruct((B, KV_WIDTH, S), BF16),
            jax.ShapeDtypeStruct((B, KV_WIDTH, S), BF16),
            jax.ShapeDtypeStruct((T, n_rest), BF16),
        ],
        compiler_params=_params(1),
        name="inproj",
    )(x2, scale1, shift1, w_in_b, cos2, sin2, qs, ks, gsum)


def _attn_kernel(q_ref, kt_ref, vt_ref, o_ref, *s_refs):
    kt = kt_ref[...]
    vt = vt_ref[...]
    tq, S = s_refs[0].shape
    kc = min(KV_CHUNK, S)
    heads = []
    for pp in range(q_ref.shape[0]):
        q2 = q_ref[pp]
        for hh in range(LANES // HEAD_DIM):
            heads.append(q2[:, hh * HEAD_DIM:(hh + 1) * HEAD_DIM])
    s_refs[0][...] = _dot(heads[0], kt)
    outs = []
    for h, s_ref in enumerate(s_refs):
        if h + 1 < len(heads):
            s_refs[h + 1][...] = _dot(heads[h + 1], kt)
        m = jnp.max(s_ref[...], axis=-1, keepdims=True)
        l = jnp.zeros_like(m)
        acc = jnp.zeros((tq, HEAD_DIM), F32)
        for c in range(S // kc):
            p = jnp.exp2(s_ref[:, c * kc:(c + 1) * kc] - m)
            l = l + jnp.sum(p, axis=-1, keepdims=True)
            acc = acc + _dot_nt(p.astype(BF16), vt[:, c * kc:(c + 1) * kc])
        outs.append(acc / l)
    o_ref[...] = jnp.concatenate(outs, axis=1).astype(BF16)


def _attention(q4, kt, vt, B, S):
    nq, T, _ = q4.shape
    tq = min(TQ, S)
    nqb = S // tq
    pairs_per_kv = nq // N_KV_HEADS
    heads_per_kv = N_Q_HEADS // N_KV_HEADS
    return pl.pallas_call(
        _attn_kernel,
        grid=(B, N_KV_HEADS, nqb),
        in_specs=[
            pl.BlockSpec((pairs_per_kv, tq, LANES), lambda b, g, i: (g, b * nqb + i, 0)),
            pl.BlockSpec((None, HEAD_DIM, S), lambda b, g, i: (b, g, 0)),
            pl.BlockSpec((None, HEAD_DIM, S), lambda b, g, i: (b, g, 0)),
        ],
        out_specs=pl.BlockSpec((tq, pairs_per_kv * LANES), lambda b, g, i: (b * nqb + i, g)),
        out_shape=jax.ShapeDtypeStruct((T, ATTN_WIDTH), BF16),
        scratch_shapes=[pltpu.VMEM((tq, S), F32)] * heads_per_kv,
        compiler_params=_params(3),
        name="attention",
    )(q4, kt, vt)


def _gelu(a):
    return 0.5 * a * (1.0 + lax.erf(a * (2.0 ** -0.5)))


def _route(sel_t, scores_t):
    tm = sel_t.shape[1]
    neg = -jnp.inf
    row8 = lax.broadcasted_iota(I32, (GROUP_SIZE, tm), 0)
    groups = [sel_t[g * GROUP_SIZE:(g + 1) * GROUP_SIZE, :] for g in range(N_GROUPS)]
    gscore = []
    for a in groups:
        m1 = jnp.max(a, axis=0, keepdims=True)
        i1 = jnp.min(jnp.where(a == m1, row8, GROUP_SIZE), axis=0, keepdims=True)
        m2 = jnp.max(jnp.where(row8 == i1, neg, a), axis=0, keepdims=True)
        gscore.append(m1 + m2)
    masked = []
    for g in range(N_GROUPS):
        rank = jnp.zeros((1, tm), I32)
        for g2 in range(N_GROUPS):
            if g2 == g:
                continue
            if g2 < g:
                ahead = gscore[g2] >= gscore[g]
            else:
                ahead = gscore[g2] > gscore[g]
            rank = rank + ahead.astype(I32)
        keep = rank < TOPK_GROUPS
        masked.append(jnp.where(keep, groups[g], neg))
    cur = jnp.concatenate(masked, axis=0)
    row = lax.broadcasted_iota(I32, (N_EXPERTS, tm), 0)
    wts, hits = [], []
    chosen = jnp.zeros((N_EXPERTS, tm), F32)
    for _ in range(TOP_K):
        m = jnp.max(cur, axis=0, keepdims=True)
        idx = jnp.min(jnp.where(cur == m, row, N_EXPERTS), axis=0, keepdims=True)
        hit = row == idx
        wts.append(jnp.sum(jnp.where(hit, scores_t, 0.0), axis=0, keepdims=True))
        hits.append(hit)
        chosen = jnp.where(hit, 1.0, chosen)
        cur = jnp.where(hit, neg, cur)
    return wts, hits, chosen


def _mix_kernel(z_ref, ga_ref, gb_ref, oa_ref, x_ref, g1_ref, sc2_ref, sh2_ref,
                sg_ref, sb_ref, wcat_ref, bexp_ref, wa_ref, wb_ref, wo_ref,
                l1g_ref, l1b_ref, rwh_ref, rwl_ref, rb_ref, tri_ref, etri_ref,
                x1_ref, h2_ref, loc_ref, wts_ref, n8_ref, *, alpha):
    tm = z_ref.shape[0]
    z = _gelu(z_ref[...].astype(F32))
    u = z[:, :SGU_WIDTH]
    v = _layer_norm(z[:, SGU_WIDTH:], sg_ref[...], sb_ref[...])
    lane = lax.broadcasted_iota(I32, (SGU_CHUNK, SGU_WIDTH), 1)
    wcat = wcat_ref[...]
    svs = []
    for c in range(tm // SGU_CHUNK):
        vc = v[c * SGU_CHUNK:(c + 1) * SGU_CHUNK, :]
        stacked = jnp.concatenate(
            [jnp.where(lane // SGU_GROUP_DIM == g, vc, 0.0) for g in range(SGU_GROUPS)],
            axis=0).astype(BF16)
        svs.append(_dot(wcat, stacked) + bexp_ref[...])
    sv = jnp.concatenate(svs, axis=0) if len(svs) > 1 else svs[0]
    ob = (u * sv).astype(BF16)
    ya = _dot(oa_ref[...], wa_ref[...])
    yb = _dot(ob, wb_ref[...])
    y = (jax.nn.sigmoid(ga_ref[...].astype(F32)) * ya
         + jax.nn.sigmoid(gb_ref[...].astype(F32)) * yb)
    out = _dot(y.astype(BF16), wo_ref[...])
    x1 = _layer_norm(alpha * x_ref[...] + g1_ref[0] * out, l1g_ref[...], l1b_ref[...])
    x1_ref[...] = x1
    h2 = x1 * (1.0 + sc2_ref[0]) + sh2_ref[0]
    h2_ref[...] = h2.astype(BF16)
    hh, hl = _split_bf16(h2)
    rwh = rwh_ref[...]
    logits_t = _dot_nt(rwh, hh) + _dot_nt(rwh, hl) + _dot_nt(rwl_ref[...], hh)
    scores_t = jax.nn.sigmoid(logits_t)
    sel_t = scores_t + rb_ref[:, 0:1]
    wts, hits, chosen = _route(sel_t, scores_t)
    wsum = wts[0]
    for w in wts[1:]:
        wsum = wsum + w
    inv = ROUTED_SCALE / wsum
    rank = _dot(chosen.astype(BF16), tri_ref[...])
    n = jnp.sum(chosen, axis=1, keepdims=True)
    n8 = jnp.floor((n + (SUBLANES - 1)) * (1.0 / SUBLANES)) * SUBLANES
    n8_wide = jnp.broadcast_to(n8, (N_EXPERTS, LANES))
    run_off = _dot(etri_ref[...], n8_wide.astype(BF16))
    slot = rank + run_off[:, 0:1]
    locs = [jnp.sum(jnp.where(h, slot, 0.0), axis=0, keepdims=True) for h in hits]
    loc_ref[...] = jnp.concatenate(locs, axis=0).astype(I32)
    wts_ref[...] = jnp.concatenate([w * inv for w in wts], axis=0)
    n8_ref[...] = n8_wide.astype(I32)


def _mix(rest, oa, x2, gate1, scale2, shift2, sg, sb, wcat, bexp, wa, wb, wo, l1g, l1b,
         rwh, rwl, rb, tri, etri, B, S, alpha):
    T, D = x2.shape
    tm = min(TM_ROUTE, S)
    npb = S // tm
    const = lambda i: (0, 0)
    modspec = pl.BlockSpec((1, 1, D), lambda i: (i // npb, 0, 0))
    full = lambda a: pl.BlockSpec(a.shape, const)
    return pl.pallas_call(
        functools.partial(_mix_kernel, alpha=alpha),
        grid=(T // tm,),
        in_specs=[
            pl.BlockSpec((tm, D), lambda i: (i, 0)),
            pl.BlockSpec((tm, D), lambda i: (i, 1)),
            pl.BlockSpec((tm, D), lambda i: (i, 2)),
            pl.BlockSpec((tm, ATTN_WIDTH), lambda i: (i, 0)),
            pl.BlockSpec((tm, D), lambda i: (i, 0)),
            modspec, modspec, modspec,
            full(sg), full(sb), full(wcat), full(bexp), full(wa), full(wb), full(wo),
            full(l1g), full(l1b), full(rwh), full(rwl), full(rb), full(tri), full(etri),
        ],
        out_specs=[
            pl.BlockSpec((tm, D), lambda i: (i, 0)),
            pl.BlockSpec((tm, D), lambda i: (i, 0)),
            pl.BlockSpec((TOP_K, tm), lambda i: (0, i)),
            pl.BlockSpec((TOP_K, tm), lambda i: (0, i)),
            pl.BlockSpec((None, N_EXPERTS, LANES), lambda i: (i, 0, 0)),
        ],
        out_shape=[
            jax.ShapeDtypeStruct((T, D), F32),
            jax.ShapeDtypeStruct((T, D), BF16),
            jax.ShapeDtypeStruct((TOP_K, T), I32),
            jax.ShapeDtypeStruct((TOP_K, T), F32),
            jax.ShapeDtypeStruct((T // tm, N_EXPERTS, LANES), I32),
        ],
        compiler_params=_params(1),
        name="mix",
    )(rest, rest, rest, oa, x2, gate1, scale2, shift2, sg, sb, wcat, bexp, wa, wb, wo,
      l1g, l1b, rwh, rwl, rb, tri, etri)


SMEM_ROW = 128


def _smem_table(a):
    return jnp.pad(a.astype(I32), ((0, 0), (0, SMEM_ROW - a.shape[1]))).reshape(-1)


def _tile_rows(tm):
    return tm * TOP_K + N_EXPERTS * SUBLANES


def _run_copy(src_ref, s, dst_ref, d, n, sem):
    return pltpu.make_async_copy(src_ref.at[pl.ds(s, n)], dst_ref.at[pl.ds(d, n)], sem)


def _for_each_run(n_ref, fn):
    def body(e, c):
        n = pl.multiple_of(n_ref[e], SUBLANES)

        @pl.when(n > 0)
        def _():
            fn(e, n)
        return c
    lax.fori_loop(0, N_EXPERTS, body, 0)


def _dispatch_kernel(n8_ref, loff_ref, gbase_ref, n8_prev_ref, h_ref, loc_ref,
                     xs_ref, xp_ref, sems):
    i = pl.program_id(0)
    rows, tm = xp_ref.shape[1], h_ref.shape[0]
    slot = i % 2

    loc = loc_ref[...]
    row = lax.broadcasted_iota(I32, (rows, tm), 0)
    perm = jnp.zeros((rows, tm), F32)
    for k in range(TOP_K):
        perm = jnp.where(row == loc[k:k + 1, :], 1.0, perm)
    xp_ref[slot] = _pack_pairs(_dot(perm.astype(BF16), h_ref[...]))

    def run_copy(e, n):
        return _run_copy(xp_ref.at[slot], pl.multiple_of(loff_ref[e], SUBLANES),
                         xs_ref, pl.multiple_of(gbase_ref[e], SUBLANES), n, sems.at[slot])

    def run_wait(s):
        return lambda e, n: _run_copy(xp_ref.at[s], 0, xs_ref, 0, n, sems.at[s]).wait()

    _for_each_run(n8_ref, lambda e, n: run_copy(e, n).start())

    @pl.when(i > 0)
    def _():
        _for_each_run(n8_prev_ref, run_wait(1 - slot))

    @pl.when(i == pl.num_programs(0) - 1)
    def _():
        _for_each_run(n8_ref, run_wait(slot))


def _dispatch(n8_flat, loff_flat, gbase_flat, h2b, loc_t, n_rows):
    T, D = h2b.shape
    tm = min(TM_ROUTE, T)
    rows = _tile_rows(tm)
    per_tile = pl.BlockSpec((SMEM_ROW,), lambda i: (i,), memory_space=pltpu.SMEM)
    prev_tile = pl.BlockSpec((SMEM_ROW,), lambda i: (jnp.maximum(i - 1, 0),),
                             memory_space=pltpu.SMEM)
    return pl.pallas_call(
        _dispatch_kernel,
        grid=(T // tm,),
        in_specs=[
            per_tile, per_tile, per_tile, prev_tile,
            pl.BlockSpec((tm, D), lambda i: (i, 0)),
            pl.BlockSpec((TOP_K, tm), lambda i: (0, i)),
        ],
        out_specs=pl.BlockSpec(memory_space=pl.ANY),
        out_shape=jax.ShapeDtypeStruct((n_rows, D // 2), U32),
        scratch_shapes=[pltpu.VMEM((2, rows, D // 2), U32), pltpu.SemaphoreType.DMA((2,))],
        compiler_params=_params(1),
        name="dispatch",
    )(n8_flat, loff_flat, gbase_flat, n8_flat, h2b, loc_t)


N_XBUF = 3
N_YBUF = 2


def _experts_kernel(gstart_ref, nch_ref, cstart_ref, crows_ref, total_ref,
                    xs_ref, wg_ref, wu_ref, wd_ref, ys_ref,
                    xbuf, ybuf, wg_s, wu_s, wd_s, sem_in, sem_out):
    e = pl.program_id(0)
    total = total_ref[0]

    def in_copy(g):
        s = g % N_XBUF
        n = pl.multiple_of(crows_ref[g], SUBLANES)
        return _run_copy(xs_ref, pl.multiple_of(cstart_ref[g], SUBLANES), xbuf.at[s], 0, n,
                         sem_in.at[s])

    def out_copy(g):
        s = g % N_YBUF
        n = pl.multiple_of(crows_ref[g], SUBLANES)
        return _run_copy(ybuf.at[s], 0, ys_ref, pl.multiple_of(cstart_ref[g], SUBLANES), n,
                         sem_out.at[s])

    def when_chunk(g, fn):
        @pl.when(jnp.logical_and(g >= 0, g < total))
        def _():
            fn(g)

    @pl.when(e == 0)
    def _():
        xbuf[...] = jnp.zeros_like(xbuf)
        for g0 in range(N_XBUF - 1):
            when_chunk(g0, lambda g: in_copy(g).start())

    wg_s[...] = wg_ref[...].astype(BF16)
    wu_s[...] = wu_ref[...].astype(BF16)
    wd_s[...] = wd_ref[...].astype(BF16)

    def chunk(c, carry):
        g = gstart_ref[e] + c
        in_copy(g).wait()
        when_chunk(g + (N_XBUF - 1), lambda g2: in_copy(g2).start())
        xb = _unpack_pairs(xbuf[g % N_XBUF])
        gate = _dot(xb, wg_s[...])
        up = _dot(xb, wu_s[...])
        hb = (gate * jax.nn.sigmoid(gate) * up).astype(BF16)
        y = _pack_pairs(_dot(hb, wd_s[...]))
        when_chunk(g - N_YBUF, lambda g2: out_copy(g2).wait())
        ybuf[g % N_YBUF] = y
        out_copy(g).start()
        return carry
    lax.fori_loop(0, nch_ref[e], chunk, 0)

    @pl.when(e == pl.num_programs(0) - 1)
    def _():
        for back in range(N_YBUF, 0, -1):
            when_chunk(total - back, lambda g: out_copy(g).wait())


def _experts(gstart, nch, cstart, crows, total, xs, w_gate, w_up, w_down, layer):
    _, E, D, FF = w_gate.shape
    wspec = lambda a, b: pl.BlockSpec((None, None, a, b), lambda e, *_: (layer, e, 0, 0))
    grid_spec = pltpu.PrefetchScalarGridSpec(
        num_scalar_prefetch=5,
        grid=(E,),
        in_specs=[pl.BlockSpec(memory_space=pl.ANY), wspec(D, FF), wspec(D, FF), wspec(FF, D)],
        out_specs=pl.BlockSpec(memory_space=pl.ANY),
        scratch_shapes=[
            pltpu.VMEM((N_XBUF, CH_EXP, D // 2), U32),
            pltpu.VMEM((N_YBUF, CH_EXP, D // 2), U32),
            pltpu.VMEM((D, FF), BF16),
            pltpu.VMEM((D, FF), BF16),
            pltpu.VMEM((FF, D), BF16),
            pltpu.SemaphoreType.DMA((N_XBUF,)),
            pltpu.SemaphoreType.DMA((N_YBUF,)),
        ],
    )
    return pl.pallas_call(
        _experts_kernel,
        grid_spec=grid_spec,
        out_shape=jax.ShapeDtypeStruct(xs.shape, U32),
        compiler_params=_params(1),
        name="experts",
    )(gstart, nch, cstart, crows, total, xs, w_gate, w_up, w_down)


def _combine_kernel(n8_ref, loff_ref, gbase_ref, n8_nxt_ref, loff_nxt_ref, gbase_nxt_ref,
                    ys_ref, loc_ref, w_ref, h_ref, x1_ref, g2_ref,
                    wsg_ref, wsu_ref, wsd_ref, l2g_ref, l2b_ref, o_ref, yp_ref, sems, *, alpha):
    i = pl.program_id(0)
    rows, tm = yp_ref.shape[1], h_ref.shape[0]
    slot = i % 2

    def fetch(lo_ref, gb_ref, s):
        def copy(e, n):
            return _run_copy(ys_ref, pl.multiple_of(gb_ref[e], SUBLANES),
                             yp_ref.at[s], pl.multiple_of(lo_ref[e], SUBLANES), n, sems.at[s])
        return copy

    @pl.when(i == 0)
    def _():
        yp_ref[...] = jnp.zeros_like(yp_ref)
        cur = fetch(loff_ref, gbase_ref, slot)
        _for_each_run(n8_ref, lambda e, n: cur(e, n).start())

    @pl.when(i + 1 < pl.num_programs(0))
    def _():
        nxt = fetch(loff_nxt_ref, gbase_nxt_ref, 1 - slot)
        _for_each_run(n8_nxt_ref, lambda e, n: nxt(e, n).start())

    hb = h_ref[...]
    g = _dot(hb, wsg_ref[...])
    u = _dot(hb, wsu_ref[...])
    y = _dot((g * jax.nn.sigmoid(g) * u).astype(BF16), wsd_ref[...])

    loc = loc_ref[...]
    w = w_ref[...]
    lane = lax.broadcasted_iota(I32, (tm, rows), 1)
    pw = jnp.zeros((tm, rows), F32)
    for k in range(TOP_K):
        pw = jnp.where(lane == loc[:, k:k + 1], w[:, k:k + 1], pw)
    pw = pw.astype(BF16)

    cur = fetch(loff_ref, gbase_ref, slot)
    _for_each_run(n8_ref, lambda e, n: cur(e, n).wait())

    yp = _unpack_pairs(yp_ref[slot])
    y = y + _dot(pw, yp)
    o_ref[...] = _layer_norm(alpha * x1_ref[...] + g2_ref[0] * y, l2g_ref[...], l2b_ref[...])


def _combine(n8_flat, loff_flat, gbase_flat, ys, loc_tk, w_tk, h2b, x1, gate2, wsg, wsu, wsd,
             l2g, l2b, B, S, alpha):
    T, D = h2b.shape
    tm = min(TM_ROUTE, S)
    npb = S // tm
    rows = _tile_rows(tm)
    const = lambda i: (0, 0)
    full = lambda a: pl.BlockSpec(a.shape, const)
    n_tiles = T // tm
    per_tile = pl.BlockSpec((SMEM_ROW,), lambda i: (i,), memory_space=pltpu.SMEM)
    next_tile = pl.BlockSpec((SMEM_ROW,), lambda i: (jnp.minimum(i + 1, n_tiles - 1),),
                             memory_space=pltpu.SMEM)
    return pl.pallas_call(
        functools.partial(_combine_kernel, alpha=alpha),
        grid=(n_tiles,),
        in_specs=[
            per_tile, per_tile, per_tile, next_tile, next_tile, next_tile,
            pl.BlockSpec(memory_space=pl.ANY),
            pl.BlockSpec((tm, TOP_K), lambda i: (i, 0)),
            pl.BlockSpec((tm, TOP_K), lambda i: (i, 0)),
            pl.BlockSpec((tm, D), lambda i: (i, 0)),
            pl.BlockSpec((tm, D), lambda i: (i, 0)),
            pl.BlockSpec((1, 1, D), lambda i: (i // npb, 0, 0)),
            full(wsg), full(wsu), full(wsd), full(l2g), full(l2b),
        ],
        out_specs=pl.BlockSpec((tm, D), lambda i: (i, 0)),
        out_shape=jax.ShapeDtypeStruct((T, D), F32),
        scratch_shapes=[pltpu.VMEM((2, rows, D // 2), U32), pltpu.SemaphoreType.DMA((2,))],
        compiler_params=_params(1),
        name="combine",
    )(n8_flat, loff_flat, gbase_flat, n8_flat, loff_flat, gbase_flat,
      ys, loc_tk, w_tk, h2b, x1, gate2, wsg, wsu, wsd, l2g, l2b)


def _rope_tables(S):
    rows = S // GRID_W
    t = jnp.arange(S)
    row = (t // GRID_W - rows // 2).astype(F32)
    col = (t % GRID_W - GRID_W // 2).astype(F32)
    n_pairs = HEAD_DIM // 4
    inv_freq = ROPE_THETA ** (-jnp.arange(n_pairs, dtype=F32) / n_pairs)
    ang = jnp.concatenate([row[:, None] * inv_freq, col[:, None] * inv_freq], -1)
    cos, sin = jnp.cos(ang), jnp.sin(ang)
    cos_i = jnp.repeat(cos, 2, axis=-1)
    sin_i = jnp.stack([-sin, sin], axis=-1).reshape(S, HEAD_DIM)
    reps = LANES // HEAD_DIM
    return jnp.tile(cos_i, (1, reps)), jnp.tile(sin_i, (1, reps))


def kernel(x, c, w_ada, b_ada, w_in, q_scale, k_scale, sgu_ln_g, sgu_ln_b, w_spatial, b_spatial,
           w_branch_a, w_branch_b, w_out, ln1_g, ln1_b, router_w, router_bias, w_gate, w_up, w_down,
           ws_gate, ws_up, ws_down, ln2_g, ln2_b):
    B, S, D = x.shape
    T = B * S
    L = w_ada.shape[0]
    alpha = (2 * L) ** 0.25
    tm = min(TM_ROUTE, S)
    n_tiles = T // tm
    n_rows = T * TOP_K + n_tiles * N_EXPERTS * SUBLANES
    max_chunks = n_rows // CH_EXP + N_EXPERTS

    cos2, sin2 = _rope_tables(S)
    mod = _modulation(c, w_ada, b_ada)
    hd = jnp.arange(ATTN_WIDTH) // HEAD_DIM
    gsum = (hd[:, None] == hd[None, :]).astype(BF16)
    tri = (jnp.arange(tm)[:, None] < jnp.arange(tm)[None, :]).astype(BF16)
    etri = (jnp.arange(N_EXPERTS)[None, :] < jnp.arange(N_EXPERTS)[:, None]).astype(BF16)

    x2 = x.reshape(T, D)
    for l in range(L):
        m6 = mod[l].reshape(B, 1, 6, D)
        shift1, scale1, gate1, shift2, scale2, gate2 = [m6[:, :, j, :] for j in range(6)]
        qs = jnp.tile(q_scale[l], N_Q_HEADS).reshape(1, ATTN_WIDTH)
        ks = jnp.tile(k_scale[l], N_KV_HEADS).reshape(1, KV_WIDTH)
        q4, kt, vt, rest = _inproj(x2, scale1, shift1, w_in[l].astype(BF16), cos2, sin2, qs, ks,
                                   gsum, B, S)
        oa = _attention(q4, kt, vt, B, S)

        wcat = jnp.transpose(w_spatial[l], (1, 0, 2)).reshape(SGU_CHUNK, SGU_GROUPS * SGU_CHUNK)
        bexp = jnp.repeat(b_spatial[l].T, SGU_GROUP_DIM, axis=1)
        rw_t = router_w[l].T
        rwh = rw_t.astype(BF16)
        rwl = (rw_t - rwh.astype(F32)).astype(BF16)
        rb = jnp.broadcast_to(router_bias[l].astype(F32)[:, None], (N_EXPERTS, LANES))
        x1, h2b, loc_t, wts_t, n8_out = _mix(
            rest, oa, x2, gate1, scale2, shift2,
            sgu_ln_g[l].reshape(1, -1), sgu_ln_b[l].reshape(1, -1), wcat.astype(BF16), bexp,
            w_branch_a[l].astype(BF16), w_branch_b[l].astype(BF16), w_out[l].astype(BF16),
            ln1_g[l].reshape(1, -1), ln1_b[l].reshape(1, -1), rwh, rwl, rb, tri, etri, B, S, alpha)

        n8 = n8_out[:, :, 0]
        tot = jnp.sum(n8, axis=0)
        pstarts = jnp.cumsum(tot) - tot
        gbase = pstarts[None, :] + jnp.cumsum(n8, axis=0) - n8
        loff = jnp.cumsum(n8, axis=1) - n8
        n8_flat, loff_flat, gbase_flat = [_smem_table(a) for a in (n8, loff, gbase)]
        nch = (tot + CH_EXP - 1) // CH_EXP
        gend = jnp.cumsum(nch)
        gstart = gend - nch
        gidx = jnp.arange(max_chunks, dtype=I32)
        owner = (gend[None, :] <= gidx[:, None]).astype(I32).sum(axis=1)
        onehot = (owner[:, None] == jnp.arange(N_EXPERTS)[None, :]).astype(I32)
        pick = lambda v: jnp.sum(onehot * v[None, :], axis=1)
        within = (gidx - pick(gstart)) * CH_EXP
        crows = jnp.clip(pick(tot) - within, 0, CH_EXP).astype(I32)
        cstart = (pick(pstarts) + within).astype(I32)
        total = gend[-1].astype(I32).reshape(1)

        xs = _dispatch(n8_flat, loff_flat, gbase_flat, h2b, loc_t, n_rows)
        ys = _experts(gstart.astype(I32), nch.astype(I32), cstart, crows, total, xs,
                      w_gate, w_up, w_down, l)
        x2 = _combine(n8_flat, loff_flat, gbase_flat, ys, loc_t.T, wts_t.T, h2b, x1, gate2,
                      ws_gate[l].astype(BF16), ws_up[l].astype(BF16), ws_down[l].astype(BF16),
                      ln2_g[l].reshape(1, -1), ln2_b[l].reshape(1, -1), B, S, alpha)
    return x2.reshape(B, S, D)
```

```python
import functools

import jax
import jax.numpy as jnp
from jax import lax
from jax.experimental import pallas as pl
from jax.experimental.pallas import tpu as pltpu

F32 = jnp.float32
BF16 = jnp.bfloat16
I32 = jnp.int32
U32 = jnp.uint32

GRID_W = 64
N_Q_HEADS = 8
N_KV_HEADS = 2
HEAD_DIM = 64
ROPE_THETA = 10000.0
ATTN_WIDTH = N_Q_HEADS * HEAD_DIM
KV_WIDTH = N_KV_HEADS * HEAD_DIM
SGU_GROUPS = 8
SGU_WIDTH = 512
SGU_GROUP_DIM = SGU_WIDTH // SGU_GROUPS
SGU_CHUNK = 128
N_EXPERTS = 64
TOP_K = 8
N_GROUPS = 8
TOPK_GROUPS = 4
GROUP_SIZE = N_EXPERTS // N_GROUPS
ROUTED_SCALE = 2.5
LN_EPS = 1e-5
RMS_EPS = 1e-6

LANES = 128
SUBLANES = 8
VMEM_LIMIT = 56 * 1024 * 1024

TM_IN = 256
TQ = 256
KV_CHUNK = 512
LOG2_E = 1.4426950408889634
TM_ROUTE = 256
CH_EXP = 512


def _params(n_axes):
    return pltpu.CompilerParams(
        dimension_semantics=("arbitrary",) * n_axes, vmem_limit_bytes=VMEM_LIMIT)


def _dot(a, b):
    return jnp.dot(a, b, preferred_element_type=F32)


def _dot_nt(a, b):
    return lax.dot_general(a, b, (((1,), (1,)), ((), ())), preferred_element_type=F32)


def _split_bf16(a):
    hi = a.astype(BF16)
    lo = (a - hi.astype(F32)).astype(BF16)
    return hi, lo


def _layer_norm(r, g, b):
    mu = jnp.mean(r, axis=-1, keepdims=True)
    d = r - mu
    var = jnp.mean(d * d, axis=-1, keepdims=True)
    return d * lax.rsqrt(var + LN_EPS) * g + b


def _pack_pairs(y):
    n = y.shape[1] // 2
    hi = lax.bitcast_convert_type(y[:, :n].astype(BF16).astype(F32), U32)
    lo = lax.bitcast_convert_type(y[:, n:].astype(BF16).astype(F32), U32)
    return hi | (lo >> 16)


def _unpack_pairs(w):
    hi = lax.bitcast_convert_type(w & jnp.uint32(0xFFFF0000), F32).astype(BF16)
    lo = lax.bitcast_convert_type(w << 16, F32).astype(BF16)
    return jnp.concatenate([hi, lo], axis=1)


def _mod_kernel(c_ref, w_ref, b_ref, o_ref):
    c = c_ref[...]
    cond = c * jax.nn.sigmoid(c)
    o_ref[...] = _dot(cond, w_ref[...]) + b_ref[...]


def _modulation(c, w_ada, b_ada):
    L, D, D6 = w_ada.shape
    B = c.shape[0]
    nj = D6 // D
    return pl.pallas_call(
        _mod_kernel,
        grid=(L, nj),
        in_specs=[
            pl.BlockSpec((B, D), lambda l, j: (0, 0)),
            pl.BlockSpec((None, D, D), lambda l, j: (l, 0, j)),
            pl.BlockSpec((None, 1, D), lambda l, j: (l, 0, j)),
        ],
        out_specs=pl.BlockSpec((None, B, D), lambda l, j: (l, 0, j)),
        out_shape=jax.ShapeDtypeStruct((L, B, D6), F32),
        compiler_params=_params(2),
        name="modulation",
    )(c, w_ada, b_ada.reshape(L, 1, D6))


def _swap_pairs(a):
    n = a.shape[1]
    lane = lax.broadcasted_iota(I32, a.shape, 1)
    nxt = pltpu.roll(a, n - 1, axis=1)
    prv = pltpu.roll(a, 1, axis=1)
    return jnp.where((lane & 1) == 0, nxt, prv)


def _head_rms(p, gsum_ref, width):
    sq = p * p
    hi, lo = _split_bf16(sq)
    g = gsum_ref[0:width, 0:width]
    ss = _dot(hi, g) + _dot(lo, g)
    return lax.rsqrt(ss * (1.0 / HEAD_DIM) + RMS_EPS)


def _inproj_kernel(x_ref, sc_ref, sh_ref, w_ref, cos_ref, sin_ref, qs_ref, ks_ref, gsum_ref,
                   q_ref, kt_ref, vt_ref, r_ref):
    x = x_ref[...]
    h = (x * (1.0 + sc_ref[0]) + sh_ref[0]).astype(BF16)
    cos2 = cos_ref[...]
    sin2 = sin_ref[...]
    pq = _dot(h, w_ref[:, 0:ATTN_WIDTH])
    qn = pq * _head_rms(pq, gsum_ref, ATTN_WIDTH) * qs_ref[...]
    cos8 = jnp.concatenate([cos2] * (ATTN_WIDTH // LANES), axis=1)
    sin8 = jnp.concatenate([sin2] * (ATTN_WIDTH // LANES), axis=1)
    qr = (qn * cos8 + _swap_pairs(qn) * sin8) * (HEAD_DIM ** -0.5 * LOG2_E)
    for p in range(ATTN_WIDTH // LANES):
        q_ref[p] = qr[:, p * LANES:(p + 1) * LANES].astype(BF16)
    k0 = ATTN_WIDTH
    pk = _dot(h, w_ref[:, k0:k0 + KV_WIDTH])
    kn = pk * _head_rms(pk, gsum_ref, KV_WIDTH) * ks_ref[...]
    kr = kn * cos2 + _swap_pairs(kn) * sin2
    kt_ref[...] = kr.T.astype(BF16)
    v0 = k0 + KV_WIDTH
    pv = _dot(h, w_ref[:, v0:v0 + KV_WIDTH])
    vt_ref[...] = pv.T.astype(BF16)
    r0 = v0 + KV_WIDTH
    n_rest = r_ref.shape[1]
    step = 1024
    for j in range(n_rest // step):
        r_ref[:, j * step:(j + 1) * step] = _dot(
            h, w_ref[:, r0 + j * step:r0 + (j + 1) * step]).astype(BF16)


def _inproj(x2, scale1, shift1, w_in_b, cos2, sin2, qs, ks, gsum, B, S):
    T, D = x2.shape
    tm = min(TM_IN, S)
    npb = S // tm
    n_rest = w_in_b.shape[1] - ATTN_WIDTH - 2 * KV_WIDTH
    nq = ATTN_WIDTH // LANES
    const = lambda i: (0, 0)
    return pl.pallas_call(
        _inproj_kernel,
        grid=(T // tm,),
        in_specs=[
            pl.BlockSpec((tm, D), lambda i: (i, 0)),
            pl.BlockSpec((1, 1, D), lambda i: (i // npb, 0, 0)),
            pl.BlockSpec((1, 1, D), lambda i: (i // npb, 0, 0)),
            pl.BlockSpec(w_in_b.shape, const),
            pl.BlockSpec((tm, LANES), lambda i: (i % npb, 0)),
            pl.BlockSpec((tm, LANES), lambda i: (i % npb, 0)),
            pl.BlockSpec((1, ATTN_WIDTH), const),
            pl.BlockSpec((1, KV_WIDTH), const),
            pl.BlockSpec(gsum.shape, const),
        ],
        out_specs=[
            pl.BlockSpec((nq, tm, LANES), lambda i: (0, i, 0)),
            pl.BlockSpec((None, KV_WIDTH, tm), lambda i: (i // npb, 0, i % npb)),
            pl.BlockSpec((None, KV_WIDTH, tm), lambda i: (i // npb, 0, i % npb)),
            pl.BlockSpec((tm, n_rest), lambda i: (i, 0)),
        ],
        out_shape=[
            jax.ShapeDtypeStruct((nq, T, LANES), BF16),
            jax.ShapeDtypeStruct((B, KV_WIDTH, S), BF16),
            jax.ShapeDtypeStruct((B, KV_WIDTH, S), BF16),
            jax.ShapeDtypeStruct((T, n_rest), BF16),
        ],
        compiler_params=_params(1),
        name="inproj",
    )(x2, scale1, shift1, w_in_b, cos2, sin2, qs, ks, gsum)


def _attn_kernel(q_ref, kt_ref, vt_ref, o_ref, *s_refs):
    kt = kt_ref[...]
    vt = vt_ref[...]
    tq, S = s_refs[0].shape
    kc = min(KV_CHUNK, S)
    heads = []
    for pp in range(q_ref.shape[0]):
        q2 = q_ref[pp]
        for hh in range(LANES // HEAD_DIM):
            heads.append(q2[:, hh * HEAD_DIM:(hh + 1) * HEAD_DIM])
    s_refs[0][...] = _dot(heads[0], kt)
    outs = []
    for h, s_ref in enumerate(s_refs):
        if h + 1 < len(heads):
            s_refs[h + 1][...] = _dot(heads[h + 1], kt)
        m = jnp.max(s_ref[...], axis=-1, keepdims=True)
        l = jnp.zeros_like(m)
        acc = jnp.zeros((tq, HEAD_DIM), F32)
        for c in range(S // kc):
            p = jnp.exp2(s_ref[:, c * kc:(c + 1) * kc] - m)
            l = l + jnp.sum(p, axis=-1, keepdims=True)
            acc = acc + _dot_nt(p.astype(BF16), vt[:, c * kc:(c + 1) * kc])
        outs.append(acc / l)
    o_ref[...] = jnp.concatenate(outs, axis=1).astype(BF16)


def _attention(q4, kt, vt, B, S):
    nq, T, _ = q4.shape
    tq = min(TQ, S)
    nqb = S // tq
    pairs_per_kv = nq // N_KV_HEADS
    heads_per_kv = N_Q_HEADS // N_KV_HEADS
    return pl.pallas_call(
        _attn_kernel,
        grid=(B, N_KV_HEADS, nqb),
        in_specs=[
            pl.BlockSpec((pairs_per_kv, tq, LANES), lambda b, g, i: (g, b * nqb + i, 0)),
            pl.BlockSpec((None, HEAD_DIM, S), lambda b, g, i: (b, g, 0)),
            pl.BlockSpec((None, HEAD_DIM, S), lambda b, g, i: (b, g, 0)),
        ],
        out_specs=pl.BlockSpec((tq, pairs_per_kv * LANES), lambda b, g, i: (b * nqb + i, g)),
        out_shape=jax.ShapeDtypeStruct((T, ATTN_WIDTH), BF16),
        scratch_shapes=[pltpu.VMEM((tq, S), F32)] * heads_per_kv,
        compiler_params=_params(3),
        name="attention",
    )(q4, kt, vt)


def _gelu(a):
    return 0.5 * a * (1.0 + lax.erf(a * (2.0 ** -0.5)))


def _route(sel_t, scores_t):
    tm = sel_t.shape[1]
    neg = -jnp.inf
    row8 = lax.broadcasted_iota(I32, (GROUP_SIZE, tm), 0)
    groups = [sel_t[g * GROUP_SIZE:(g + 1) * GROUP_SIZE, :] for g in range(N_GROUPS)]
    gscore = []
    for a in groups:
        m1 = jnp.max(a, axis=0, keepdims=True)
        i1 = jnp.min(jnp.where(a == m1, row8, GROUP_SIZE), axis=0, keepdims=True)
        m2 = jnp.max(jnp.where(row8 == i1, neg, a), axis=0, keepdims=True)
        gscore.append(m1 + m2)
    masked = []
    for g in range(N_GROUPS):
        rank = jnp.zeros((1, tm), I32)
        for g2 in range(N_GROUPS):
            if g2 == g:
                continue
            if g2 < g:
                ahead = gscore[g2] >= gscore[g]
            else:
                ahead = gscore[g2] > gscore[g]
            rank = rank + ahead.astype(I32)
        keep = rank < TOPK_GROUPS
        masked.append(jnp.where(keep, groups[g], neg))
    cur = jnp.concatenate(masked, axis=0)
    row = lax.broadcasted_iota(I32, (N_EXPERTS, tm), 0)
    wts, hits = [], []
    chosen = jnp.zeros((N_EXPERTS, tm), F32)
    for _ in range(TOP_K):
        m = jnp.max(cur, axis=0, keepdims=True)
        idx = jnp.min(jnp.where(cur == m, row, N_EXPERTS), axis=0, keepdims=True)
        hit = row == idx
        wts.append(jnp.sum(jnp.where(hit, scores_t, 0.0), axis=0, keepdims=True))
        hits.append(hit)
        chosen = jnp.where(hit, 1.0, chosen)
        cur = jnp.where(hit, neg, cur)
    return wts, hits, chosen


def _mix_kernel(z_ref, ga_ref, gb_ref, oa_ref, x_ref, g1_ref, sc2_ref, sh2_ref,
                sg_ref, sb_ref, wcat_ref, bexp_ref, wa_ref, wb_ref, wo_ref,
                l1g_ref, l1b_ref, rwh_ref, rwl_ref, rb_ref, tri_ref, etri_ref,
                x1_ref, h2_ref, loc_ref, wts_ref, n8_ref, *, alpha):
    tm = z_ref.shape[0]
    z = _gelu(z_ref[...].astype(F32))
    u = z[:, :SGU_WIDTH]
    v = _layer_norm(z[:, SGU_WIDTH:], sg_ref[...], sb_ref[...])
    lane = lax.broadcasted_iota(I32, (SGU_CHUNK, SGU_WIDTH), 1)
    wcat = wcat_ref[...]
    svs = []
    for c in range(tm // SGU_CHUNK):
        vc = v[c * SGU_CHUNK:(c + 1) * SGU_CHUNK, :]
        stacked = jnp.concatenate(
            [jnp.where(lane // SGU_GROUP_DIM == g, vc, 0.0) for g in range(SGU_GROUPS)],
            axis=0).astype(BF16)
        svs.append(_dot(wcat, stacked) + bexp_ref[...])
    sv = jnp.concatenate(svs, axis=0) if len(svs) > 1 else svs[0]
    ob = (u * sv).astype(BF16)
    ya = _dot(oa_ref[...], wa_ref[...])
    yb = _dot(ob, wb_ref[...])
    y = (jax.nn.sigmoid(ga_ref[...].astype(F32)) * ya
         + jax.nn.sigmoid(gb_ref[...].astype(F32)) * yb)
    out = _dot(y.astype(BF16), wo_ref[...])
    x1 = _layer_norm(alpha * x_ref[...] + g1_ref[0] * out, l1g_ref[...], l1b_ref[...])
    x1_ref[...] = x1
    h2 = x1 * (1.0 + sc2_ref[0]) + sh2_ref[0]
    h2_ref[...] = h2.astype(BF16)
    hh, hl = _split_bf16(h2)
    rwh = rwh_ref[...]
    logits_t = _dot_nt(rwh, hh) + _dot_nt(rwh, hl) + _dot_nt(rwl_ref[...], hh)
    scores_t = jax.nn.sigmoid(logits_t)
    sel_t = scores_t + rb_ref[:, 0:1]
    wts, hits, chosen = _route(sel_t, scores_t)
    wsum = wts[0]
    for w in wts[1:]:
        wsum = wsum + w
    inv = ROUTED_SCALE / wsum
    rank = _dot(chosen.astype(BF16), tri_ref[...])
    n = jnp.sum(chosen, axis=1, keepdims=True)
    n8 = jnp.floor((n + (SUBLANES - 1)) * (1.0 / SUBLANES)) * SUBLANES
    n8_wide = jnp.broadcast_to(n8, (N_EXPERTS, LANES))
    run_off = _dot(etri_ref[...], n8_wide.astype(BF16))
    slot = rank + run_off[:, 0:1]
    locs = [jnp.sum(jnp.where(h, slot, 0.0), axis=0, keepdims=True) for h in hits]
    loc_ref[...] = jnp.concatenate(locs, axis=0).astype(I32)
    wts_ref[...] = jnp.concatenate([w * inv for w in wts], axis=0)
    n8_ref[...] = n8_wide.astype(I32)


def _mix(rest, oa, x2, gate1, scale2, shift2, sg, sb, wcat, bexp, wa, wb, wo, l1g, l1b,
         rwh, rwl, rb, tri, etri, B, S, alpha):
    T, D = x2.shape
    tm = min(TM_ROUTE, S)
    npb = S // tm
    const = lambda i: (0, 0)
    modspec = pl.BlockSpec((1, 1, D), lambda i: (i // npb, 0, 0))
    full = lambda a: pl.BlockSpec(a.shape, const)
    return pl.pallas_call(
        functools.partial(_mix_kernel, alpha=alpha),
        grid=(T // tm,),
        in_specs=[
            pl.BlockSpec((tm, D), lambda i: (i, 0)),
            pl.BlockSpec((tm, D), lambda i: (i, 1)),
            pl.BlockSpec((tm, D), lambda i: (i, 2)),
            pl.BlockSpec((tm, ATTN_WIDTH), lambda i: (i, 0)),
            pl.BlockSpec((tm, D), lambda i: (i, 0)),
            modspec, modspec, modspec,
            full(sg), full(sb), full(wcat), full(bexp), full(wa), full(wb), full(wo),
            full(l1g), full(l1b), full(rwh), full(rwl), full(rb), full(tri), full(etri),
        ],
        out_specs=[
            pl.BlockSpec((tm, D), lambda i: (i, 0)),
            pl.BlockSpec((tm, D), lambda i: (i, 0)),
            pl.BlockSpec((TOP_K, tm), lambda i: (0, i)),
            pl.BlockSpec((TOP_K, tm), lambda i: (0, i)),
            pl.BlockSpec((None, N_EXPERTS, LANES), lambda i: (i, 0, 0)),
        ],
        out_shape=[
            jax.ShapeDtypeStruct((T, D), F32),
            jax.ShapeDtypeStruct((T, D), BF16),
            jax.ShapeDtypeStruct((TOP_K, T), I32),
            jax.ShapeDtypeStruct((TOP_K, T), F32),
            jax.ShapeDtypeStruct((T // tm, N_EXPERTS, LANES), I32),
        ],
        compiler_params=_params(1),
        name="mix",
    )(rest, rest, rest, oa, x2, gate1, scale2, shift2, sg, sb, wcat, bexp, wa, wb, wo,
      l1g, l1b, rwh, rwl, rb, tri, etri)


SMEM_ROW = 128


def _smem_table(a):
    a = a.astype(I32)
    a = jnp.concatenate([a, jnp.sum(a, axis=1, keepdims=True)], axis=1)
    return jnp.pad(a, ((0, 0), (0, SMEM_ROW - a.shape[1]))).reshape(-1)


def _wait_all_runs(n_ref, vmem_ref, hbm_ref, sem):
    total = pl.multiple_of(n_ref[N_EXPERTS], SUBLANES)
    _run_copy(vmem_ref, 0, hbm_ref, 0, total, sem).wait()


def _tile_rows(tm):
    return tm * TOP_K + N_EXPERTS * SUBLANES


def _run_copy(src_ref, s, dst_ref, d, n, sem):
    return pltpu.make_async_copy(src_ref.at[pl.ds(s, n)], dst_ref.at[pl.ds(d, n)], sem)


def _for_each_run(n_ref, fn):
    for e in range(N_EXPERTS):
        n = pl.multiple_of(n_ref[e], SUBLANES)

        @pl.when(n > 0)
        def _():
            fn(e, n)


def _dispatch_kernel(n8_ref, loff_ref, gbase_ref, n8_prev_ref, h_ref, loc_ref,
                     xs_ref, xp_ref, sems):
    i = pl.program_id(0)
    rows, tm = xp_ref.shape[1], h_ref.shape[0]
    slot = i % 2

    loc = loc_ref[...]
    row = lax.broadcasted_iota(I32, (rows, tm), 0)
    perm = jnp.zeros((rows, tm), F32)
    for k in range(TOP_K):
        perm = jnp.where(row == loc[k:k + 1, :], 1.0, perm)
    xp = _dot(perm.astype(BF16), h_ref[...])
    half = xp.shape[1] // 2
    xp_ref[slot] = (lax.bitcast_convert_type(xp[:, :half], U32)
                    | (lax.bitcast_convert_type(xp[:, half:], U32) >> 16))

    def run_copy(e, n):
        return _run_copy(xp_ref.at[slot], pl.multiple_of(loff_ref[e], SUBLANES),
                         xs_ref, pl.multiple_of(gbase_ref[e], SUBLANES), n, sems.at[slot])

    _for_each_run(n8_ref, lambda e, n: run_copy(e, n).start())

    @pl.when(i > 0)
    def _():
        _wait_all_runs(n8_prev_ref, xp_ref.at[1 - slot], xs_ref, sems.at[1 - slot])

    @pl.when(i == pl.num_programs(0) - 1)
    def _():
        _wait_all_runs(n8_ref, xp_ref.at[slot], xs_ref, sems.at[slot])


def _dispatch(n8_flat, loff_flat, gbase_flat, h2b, loc_t, n_rows):
    T, D = h2b.shape
    tm = min(TM_ROUTE, T)
    rows = _tile_rows(tm)
    per_tile = pl.BlockSpec((SMEM_ROW,), lambda i: (i,), memory_space=pltpu.SMEM)
    prev_tile = pl.BlockSpec((SMEM_ROW,), lambda i: (jnp.maximum(i - 1, 0),),
                             memory_space=pltpu.SMEM)
    return pl.pallas_call(
        _dispatch_kernel,
        grid=(T // tm,),
        in_specs=[
            per_tile, per_tile, per_tile, prev_tile,
            pl.BlockSpec((tm, D), lambda i: (i, 0)),
            pl.BlockSpec((TOP_K, tm), lambda i: (0, i)),
        ],
        out_specs=pl.BlockSpec(memory_space=pl.ANY),
        out_shape=jax.ShapeDtypeStruct((n_rows, D // 2), U32),
        scratch_shapes=[pltpu.VMEM((2, rows, D // 2), U32), pltpu.SemaphoreType.DMA((2,))],
        compiler_params=_params(1),
        name="dispatch",
    )(n8_flat, loff_flat, gbase_flat, n8_flat, h2b, loc_t)


N_XBUF = 3
N_YBUF = 2


def _experts_kernel(gstart_ref, nch_ref, cstart_ref, crows_ref, total_ref,
                    xs_ref, wg_ref, wu_ref, wd_ref, ys_ref,
                    xbuf, ybuf, wg_s, wu_s, wd_s, sem_in, sem_out):
    e = pl.program_id(0)
    total = total_ref[0]

    def in_copy(g):
        s = g % N_XBUF
        n = pl.multiple_of(crows_ref[g], SUBLANES)
        return _run_copy(xs_ref, pl.multiple_of(cstart_ref[g], SUBLANES), xbuf.at[s], 0, n,
                         sem_in.at[s])

    def out_copy(g):
        s = g % N_YBUF
        n = pl.multiple_of(crows_ref[g], SUBLANES)
        return _run_copy(ybuf.at[s], 0, ys_ref, pl.multiple_of(cstart_ref[g], SUBLANES), n,
                         sem_out.at[s])

    def when_chunk(g, fn):
        @pl.when(jnp.logical_and(g >= 0, g < total))
        def _():
            fn(g)

    @pl.when(e == 0)
    def _():
        xbuf[...] = jnp.zeros_like(xbuf)
        for g0 in range(N_XBUF - 1):
            when_chunk(g0, lambda g: in_copy(g).start())

    wg_s[...] = wg_ref[...].astype(BF16)
    wu_s[...] = wu_ref[...].astype(BF16)
    wd_s[...] = wd_ref[...].astype(BF16)

    def chunk(c, carry):
        g = gstart_ref[e] + c
        in_copy(g).wait()
        when_chunk(g + (N_XBUF - 1), lambda g2: in_copy(g2).start())
        xb = _unpack_pairs(xbuf[g % N_XBUF])
        gate = _dot(xb, wg_s[...])
        up = _dot(xb, wu_s[...])
        hb = (gate * jax.nn.sigmoid(gate) * up).astype(BF16)
        y = _pack_pairs(_dot(hb, wd_s[...]))
        when_chunk(g - N_YBUF, lambda g2: out_copy(g2).wait())
        ybuf[g % N_YBUF] = y
        out_copy(g).start()
        return carry
    lax.fori_loop(0, nch_ref[e], chunk, 0)

    @pl.when(e == pl.num_programs(0) - 1)
    def _():
        for back in range(N_YBUF, 0, -1):
            when_chunk(total - back, lambda g: out_copy(g).wait())


def _experts(gstart, nch, cstart, crows, total, xs, w_gate, w_up, w_down, layer):
    _, E, D, FF = w_gate.shape
    wspec = lambda a, b: pl.BlockSpec((None, None, a, b), lambda e, *_: (layer, e, 0, 0))
    grid_spec = pltpu.PrefetchScalarGridSpec(
        num_scalar_prefetch=5,
        grid=(E,),
        in_specs=[pl.BlockSpec(memory_space=pl.ANY), wspec(D, FF), wspec(D, FF), wspec(FF, D)],
        out_specs=pl.BlockSpec(memory_space=pl.ANY),
        scratch_shapes=[
            pltpu.VMEM((N_XBUF, CH_EXP, D // 2), U32),
            pltpu.VMEM((N_YBUF, CH_EXP, D // 2), U32),
            pltpu.VMEM((D, FF), BF16),
            pltpu.VMEM((D, FF), BF16),
            pltpu.VMEM((FF, D), BF16),
            pltpu.SemaphoreType.DMA((N_XBUF,)),
            pltpu.SemaphoreType.DMA((N_YBUF,)),
        ],
    )
    return pl.pallas_call(
        _experts_kernel,
        grid_spec=grid_spec,
        out_shape=jax.ShapeDtypeStruct(xs.shape, U32),
        compiler_params=_params(1),
        name="experts",
    )(gstart, nch, cstart, crows, total, xs, w_gate, w_up, w_down)


def _combine_kernel(n8_ref, loff_ref, gbase_ref, n8_nxt_ref, loff_nxt_ref, gbase_nxt_ref,
                    ys_ref, loc_ref, w_ref, h_ref, x1_ref, g2_ref,
                    wsg_ref, wsu_ref, wsd_ref, l2g_ref, l2b_ref, o_ref, yp_ref, sems, *, alpha):
    i = pl.program_id(0)
    rows, tm = yp_ref.shape[1], h_ref.shape[0]
    slot = i % 2

    def fetch(lo_ref, gb_ref, s):
        def copy(e, n):
            return _run_copy(ys_ref, pl.multiple_of(gb_ref[e], SUBLANES),
                             yp_ref.at[s], pl.multiple_of(lo_ref[e], SUBLANES), n, sems.at[s])
        return copy

    @pl.when(i == 0)
    def _():
        yp_ref[...] = jnp.zeros_like(yp_ref)
        cur = fetch(loff_ref, gbase_ref, slot)
        _for_each_run(n8_ref, lambda e, n: cur(e, n).start())

    @pl.when(i + 1 < pl.num_programs(0))
    def _():
        nxt = fetch(loff_nxt_ref, gbase_nxt_ref, 1 - slot)
        _for_each_run(n8_nxt_ref, lambda e, n: nxt(e, n).start())

    hb = h_ref[...]
    g = _dot(hb, wsg_ref[...])
    u = _dot(hb, wsu_ref[...])
    y = _dot((g * jax.nn.sigmoid(g) * u).astype(BF16), wsd_ref[...])

    loc = loc_ref[...]
    w = w_ref[...]
    lane = lax.broadcasted_iota(I32, (tm, rows), 1)
    pw = jnp.zeros((tm, rows), F32)
    for k in range(TOP_K):
        pw = jnp.where(lane == loc[:, k:k + 1], w[:, k:k + 1], pw)
    pw = pw.astype(BF16)

    _wait_all_runs(n8_ref, yp_ref.at[slot], ys_ref, sems.at[slot])

    yp = _unpack_pairs(yp_ref[slot])
    y = y + _dot(pw, yp)
    o_ref[...] = _layer_norm(alpha * x1_ref[...] + g2_ref[0] * y, l2g_ref[...], l2b_ref[...])


def _combine(n8_flat, loff_flat, gbase_flat, ys, loc_tk, w_tk, h2b, x1, gate2, wsg, wsu, wsd,
             l2g, l2b, B, S, alpha):
    T, D = h2b.shape
    tm = min(TM_ROUTE, S)
    npb = S // tm
    rows = _tile_rows(tm)
    const = lambda i: (0, 0)
    full = lambda a: pl.BlockSpec(a.shape, const)
    n_tiles = T // tm
    per_tile = pl.BlockSpec((SMEM_ROW,), lambda i: (i,), memory_space=pltpu.SMEM)
    next_tile = pl.BlockSpec((SMEM_ROW,), lambda i: (jnp.minimum(i + 1, n_tiles - 1),),
                             memory_space=pltpu.SMEM)
    return pl.pallas_call(
        functools.partial(_combine_kernel, alpha=alpha),
        grid=(n_tiles,),
        in_specs=[
            per_tile, per_tile, per_tile, next_tile, next_tile, next_tile,
            pl.BlockSpec(memory_space=pl.ANY),
            pl.BlockSpec((tm, TOP_K), lambda i: (i, 0)),
            pl.BlockSpec((tm, TOP_K), lambda i: (i, 0)),
            pl.BlockSpec((tm, D), lambda i: (i, 0)),
            pl.BlockSpec((tm, D), lambda i: (i, 0)),
            pl.BlockSpec((1, 1, D), lambda i: (i // npb, 0, 0)),
            full(wsg), full(wsu), full(wsd), full(l2g), full(l2b),
        ],
        out_specs=pl.BlockSpec((tm, D), lambda i: (i, 0)),
        out_shape=jax.ShapeDtypeStruct((T, D), F32),
        scratch_shapes=[pltpu.VMEM((2, rows, D // 2), U32), pltpu.SemaphoreType.DMA((2,))],
        compiler_params=_params(1),
        name="combine",
    )(n8_flat, loff_flat, gbase_flat, n8_flat, loff_flat, gbase_flat,
      ys, loc_tk, w_tk, h2b, x1, gate2, wsg, wsu, wsd, l2g, l2b)


def _rope_tables(S):
    rows = S // GRID_W
    t = jnp.arange(S)
    row = (t // GRID_W - rows // 2).astype(F32)
    col = (t % GRID_W - GRID_W // 2).astype(F32)
    n_pairs = HEAD_DIM // 4
    inv_freq = ROPE_THETA ** (-jnp.arange(n_pairs, dtype=F32) / n_pairs)
    ang = jnp.concatenate([row[:, None] * inv_freq, col[:, None] * inv_freq], -1)
    cos, sin = jnp.cos(ang), jnp.sin(ang)
    cos_i = jnp.repeat(cos, 2, axis=-1)
    sin_i = jnp.stack([-sin, sin], axis=-1).reshape(S, HEAD_DIM)
    reps = LANES // HEAD_DIM
    return jnp.tile(cos_i, (1, reps)), jnp.tile(sin_i, (1, reps))


def kernel(x, c, w_ada, b_ada, w_in, q_scale, k_scale, sgu_ln_g, sgu_ln_b, w_spatial, b_spatial,
           w_branch_a, w_branch_b, w_out, ln1_g, ln1_b, router_w, router_bias, w_gate, w_up, w_down,
           ws_gate, ws_up, ws_down, ln2_g, ln2_b):
    B, S, D = x.shape
    T = B * S
    L = w_ada.shape[0]
    alpha = (2 * L) ** 0.25
    tm = min(TM_ROUTE, S)
    n_tiles = T // tm
    n_rows = T * TOP_K + n_tiles * N_EXPERTS * SUBLANES
    max_chunks = n_rows // CH_EXP + N_EXPERTS

    cos2, sin2 = _rope_tables(S)
    mod = _modulation(c, w_ada, b_ada)
    hd = jnp.arange(ATTN_WIDTH) // HEAD_DIM
    gsum = (hd[:, None] == hd[None, :]).astype(BF16)
    tri = (jnp.arange(tm)[:, None] < jnp.arange(tm)[None, :]).astype(BF16)
    etri = (jnp.arange(N_EXPERTS)[None, :] < jnp.arange(N_EXPERTS)[:, None]).astype(BF16)

    x2 = x.reshape(T, D)
    for l in range(L):
        m6 = mod[l].reshape(B, 1, 6, D)
        shift1, scale1, gate1, shift2, scale2, gate2 = [m6[:, :, j, :] for j in range(6)]
        qs = jnp.tile(q_scale[l], N_Q_HEADS).reshape(1, ATTN_WIDTH)
        ks = jnp.tile(k_scale[l], N_KV_HEADS).reshape(1, KV_WIDTH)
        q4, kt, vt, rest = _inproj(x2, scale1, shift1, w_in[l].astype(BF16), cos2, sin2, qs, ks,
                                   gsum, B, S)
        oa = _attention(q4, kt, vt, B, S)

        wcat = jnp.transpose(w_spatial[l], (1, 0, 2)).reshape(SGU_CHUNK, SGU_GROUPS * SGU_CHUNK)
        bexp = jnp.repeat(b_spatial[l].T, SGU_GROUP_DIM, axis=1)
        rw_t = router_w[l].T
        rwh = rw_t.astype(BF16)
        rwl = (rw_t - rwh.astype(F32)).astype(BF16)
        rb = jnp.broadcast_to(router_bias[l].astype(F32)[:, None], (N_EXPERTS, LANES))
        x1, h2b, loc_t, wts_t, n8_out = _mix(
            rest, oa, x2, gate1, scale2, shift2,
            sgu_ln_g[l].reshape(1, -1), sgu_ln_b[l].reshape(1, -1), wcat.astype(BF16), bexp,
            w_branch_a[l].astype(BF16), w_branch_b[l].astype(BF16), w_out[l].astype(BF16),
            ln1_g[l].reshape(1, -1), ln1_b[l].reshape(1, -1), rwh, rwl, rb, tri, etri, B, S, alpha)

        n8 = n8_out[:, :, 0]
        tot = jnp.sum(n8, axis=0)
        pstarts = jnp.cumsum(tot) - tot
        gbase = pstarts[None, :] + jnp.cumsum(n8, axis=0) - n8
        loff = jnp.cumsum(n8, axis=1) - n8
        n8_flat, loff_flat, gbase_flat = [_smem_table(a) for a in (n8, loff, gbase)]
        nch = (tot + CH_EXP - 1) // CH_EXP
        gend = jnp.cumsum(nch)
        gstart = gend - nch
        gidx = jnp.arange(max_chunks, dtype=I32)
        owner = (gend[None, :] <= gidx[:, None]).astype(I32).sum(axis=1)
        onehot = (owner[:, None] == jnp.arange(N_EXPERTS)[None, :]).astype(I32)
        pick = lambda v: jnp.sum(onehot * v[None, :], axis=1)
        within = (gidx - pick(gstart)) * CH_EXP
        crows = jnp.clip(pick(tot) - within, 0, CH_EXP).astype(I32)
        cstart = (pick(pstarts) + within).astype(I32)
        total = gend[-1].astype(I32).reshape(1)

        xs = _dispatch(n8_flat, loff_flat, gbase_flat, h2b, loc_t, n_rows)
        ys = _experts(gstart.astype(I32), nch.astype(I32), cstart, crows, total, xs,
                      w_gate, w_up, w_down, l)
        x2 = _combine(n8_flat, loff_flat, gbase_flat, ys, loc_t.T, wts_t.T, h2b, x1, gate2,
                      ws_gate[l].astype(BF16), ws_up[l].astype(BF16), ws_down[l].astype(BF16),
                      ln2_g[l].reshape(1, -1), ln2_b[l].reshape(1, -1), B, S, alpha)
    return x2.reshape(B, S, D)
```

```python
import functools

import jax
import jax.numpy as jnp
from jax import lax
from jax.experimental import pallas as pl
from jax.experimental.pallas import tpu as pltpu

F32 = jnp.float32
BF16 = jnp.bfloat16
I32 = jnp.int32
I16 = jnp.int16
U32 = jnp.uint32

GRID_W = 64
N_Q_HEADS = 8
N_KV_HEADS = 2
HEAD_DIM = 64
ROPE_THETA = 10000.0
ATTN_WIDTH = N_Q_HEADS * HEAD_DIM
KV_WIDTH = N_KV_HEADS * HEAD_DIM
SGU_GROUPS = 8
SGU_WIDTH = 512
SGU_GROUP_DIM = SGU_WIDTH // SGU_GROUPS
SGU_CHUNK = 128
N_EXPERTS = 64
TOP_K = 8
N_GROUPS = 8
TOPK_GROUPS = 4
GROUP_SIZE = N_EXPERTS // N_GROUPS
ROUTED_SCALE = 2.5
LN_EPS = 1e-5
RMS_EPS = 1e-6

LANES = 128
SUBLANES = 8
VMEM_LIMIT = 56 * 1024 * 1024

TM_IN = 256
TQ = 256
KV_CHUNK = 512
LOG2_E = 1.4426950408889634
TM_ROUTE = 256
CH_EXP = 512
PERM_BLOCK = 512


def _params(n_axes):
    return pltpu.CompilerParams(
        dimension_semantics=("arbitrary",) * n_axes, vmem_limit_bytes=VMEM_LIMIT)


def _dot(a, b):
    return jnp.dot(a, b, preferred_element_type=F32)


def _dot_nt(a, b):
    return lax.dot_general(a, b, (((1,), (1,)), ((), ())), preferred_element_type=F32)


def _split_bf16(a):
    hi = a.astype(BF16)
    lo = (a - hi.astype(F32)).astype(BF16)
    return hi, lo


def _layer_norm(r, g, b):
    mu = jnp.mean(r, axis=-1, keepdims=True)
    d = r - mu
    var = jnp.mean(d * d, axis=-1, keepdims=True)
    return d * lax.rsqrt(var + LN_EPS) * g + b


def _pack_pairs(y):
    n = y.shape[1] // 2
    hi = lax.bitcast_convert_type(y[:, :n].astype(BF16).astype(F32), U32)
    lo = lax.bitcast_convert_type(y[:, n:].astype(BF16).astype(F32), U32)
    return hi | (lo >> 16)


def _unpack_pairs(w):
    hi = lax.bitcast_convert_type(w & jnp.uint32(0xFFFF0000), F32).astype(BF16)
    lo = lax.bitcast_convert_type(w << 16, F32).astype(BF16)
    return jnp.concatenate([hi, lo], axis=1)


def _mod_kernel(c_ref, w_ref, b_ref, o_ref):
    c = c_ref[...]
    cond = c * jax.nn.sigmoid(c)
    o_ref[...] = _dot(cond, w_ref[...]) + b_ref[...]


def _modulation(c, w_ada, b_ada):
    L, D, D6 = w_ada.shape
    B = c.shape[0]
    nj = D6 // D
    return pl.pallas_call(
        _mod_kernel,
        grid=(L, nj),
        in_specs=[
            pl.BlockSpec((B, D), lambda l, j: (0, 0)),
            pl.BlockSpec((None, D, D), lambda l, j: (l, 0, j)),
            pl.BlockSpec((None, 1, D), lambda l, j: (l, 0, j)),
        ],
        out_specs=pl.BlockSpec((None, B, D), lambda l, j: (l, 0, j)),
        out_shape=jax.ShapeDtypeStruct((L, B, D6), F32),
        compiler_params=_params(2),
        name="modulation",
    )(c, w_ada, b_ada.reshape(L, 1, D6))


def _swap_pairs(a):
    n = a.shape[1]
    lane = lax.broadcasted_iota(I32, a.shape, 1)
    nxt = pltpu.roll(a, n - 1, axis=1)
    prv = pltpu.roll(a, 1, axis=1)
    return jnp.where((lane & 1) == 0, nxt, prv)


def _head_rms(p, gsum_ref, width):
    sq = p * p
    hi, lo = _split_bf16(sq)
    g = gsum_ref[0:width, 0:width]
    ss = _dot(hi, g) + _dot(lo, g)
    return lax.rsqrt(ss * (1.0 / HEAD_DIM) + RMS_EPS)


def _inproj_kernel(x_ref, sc_ref, sh_ref, w_ref, cos_ref, sin_ref, qs_ref, ks_ref, gsum_ref,
                   q_ref, kt_ref, vt_ref, r_ref):
    x = x_ref[...]
    h = (x * (1.0 + sc_ref[0]) + sh_ref[0]).astype(BF16)
    cos2 = cos_ref[...]
    sin2 = sin_ref[...]
    pq = _dot(h, w_ref[:, 0:ATTN_WIDTH])
    qn = pq * _head_rms(pq, gsum_ref, ATTN_WIDTH) * qs_ref[...]
    cos8 = jnp.concatenate([cos2] * (ATTN_WIDTH // LANES), axis=1)
    sin8 = jnp.concatenate([sin2] * (ATTN_WIDTH // LANES), axis=1)
    qr = (qn * cos8 + _swap_pairs(qn) * sin8) * (HEAD_DIM ** -0.5 * LOG2_E)
    for p in range(ATTN_WIDTH // LANES):
        q_ref[p] = qr[:, p * LANES:(p + 1) * LANES].astype(BF16)
    k0 = ATTN_WIDTH
    pk = _dot(h, w_ref[:, k0:k0 + KV_WIDTH])
    kn = pk * _head_rms(pk, gsum_ref, KV_WIDTH) * ks_ref[...]
    kr = kn * cos2 + _swap_pairs(kn) * sin2
    kt_ref[...] = kr.T.astype(BF16)
    v0 = k0 + KV_WIDTH
    pv = _dot(h, w_ref[:, v0:v0 + KV_WIDTH])
    vt_ref[...] = pv.T.astype(BF16)
    r0 = v0 + KV_WIDTH
    n_rest = r_ref.shape[1]
    step = 1024
    for j in range(n_rest // step):
        r_ref[:, j * step:(j + 1) * step] = _dot(
            h, w_ref[:, r0 + j * step:r0 + (j + 1) * step]).astype(BF16)


def _inproj(x2, scale1, shift1, w_in_b, cos2, sin2, qs, ks, gsum, B, S):
    T, D = x2.shape
    tm = min(TM_IN, S)
    npb = S // tm
    n_rest = w_in_b.shape[1] - ATTN_WIDTH - 2 * KV_WIDTH
    nq = ATTN_WIDTH // LANES
    const = lambda i: (0, 0)
    return pl.pallas_call(
        _inproj_kernel,
        grid=(T // tm,),
        in_specs=[
            pl.BlockSpec((tm, D), lambda i: (i, 0)),
            pl.BlockSpec((1, 1, D), lambda i: (i // npb, 0, 0)),
            pl.BlockSpec((1, 1, D), lambda i: (i // npb, 0, 0)),
            pl.BlockSpec(w_in_b.shape, const),
            pl.BlockSpec((tm, LANES), lambda i: (i % npb, 0)),
            pl.BlockSpec((tm, LANES), lambda i: (i % npb, 0)),
            pl.BlockSpec((1, ATTN_WIDTH), const),
            pl.BlockSpec((1, KV_WIDTH), const),
            pl.BlockSpec(gsum.shape, const),
        ],
        out_specs=[
            pl.BlockSpec((nq, tm, LANES), lambda i: (0, i, 0)),
            pl.BlockSpec((None, KV_WIDTH, tm), lambda i: (i // npb, 0, i % npb)),
            pl.BlockSpec((None, KV_WIDTH, tm), lambda i: (i // npb, 0, i % npb)),
            pl.BlockSpec((tm, n_rest), lambda i: (i, 0)),
        ],
        out_shape=[
            jax.ShapeDtypeStruct((nq, T, LANES), BF16),
            jax.ShapeDtypeStruct((B, KV_WIDTH, S), BF16),
            jax.ShapeDtypeStruct((B, KV_WIDTH, S), BF16),
            jax.ShapeDtypeStruct((T, n_rest), BF16),
        ],
        compiler_params=_params(1),
        name="inproj",
    )(x2, scale1, shift1, w_in_b, cos2, sin2, qs, ks, gsum)


def _attn_kernel(q_ref, kt_ref, vt_ref, o_ref, *s_refs):
    kt = kt_ref[...]
    vt = vt_ref[...]
    tq, S = s_refs[0].shape
    kc = min(KV_CHUNK, S)
    heads = []
    for pp in range(q_ref.shape[0]):
        q2 = q_ref[pp]
        for hh in range(LANES // HEAD_DIM):
            heads.append(q2[:, hh * HEAD_DIM:(hh + 1) * HEAD_DIM])
    s_refs[0][...] = _dot(heads[0], kt)
    outs = []
    for h, s_ref in enumerate(s_refs):
        if h + 1 < len(heads):
            s_refs[h + 1][...] = _dot(heads[h + 1], kt)
        m = jnp.max(s_ref[...], axis=-1, keepdims=True)
        l = jnp.zeros_like(m)
        acc = jnp.zeros((tq, HEAD_DIM), F32)
        for c in range(S // kc):
            p = jnp.exp2(s_ref[:, c * kc:(c + 1) * kc] - m)
            l = l + jnp.sum(p, axis=-1, keepdims=True)
            acc = acc + _dot_nt(p.astype(BF16), vt[:, c * kc:(c + 1) * kc])
        outs.append(acc / l)
    o_ref[...] = jnp.concatenate(outs, axis=1).astype(BF16)


def _attention(q4, kt, vt, B, S):
    nq, T, _ = q4.shape
    tq = min(TQ, S)
    nqb = S // tq
    pairs_per_kv = nq // N_KV_HEADS
    heads_per_kv = N_Q_HEADS // N_KV_HEADS
    return pl.pallas_call(
        _attn_kernel,
        grid=(B, N_KV_HEADS, nqb),
        in_specs=[
            pl.BlockSpec((pairs_per_kv, tq, LANES), lambda b, g, i: (g, b * nqb + i, 0)),
            pl.BlockSpec((None, HEAD_DIM, S), lambda b, g, i: (b, g, 0)),
            pl.BlockSpec((None, HEAD_DIM, S), lambda b, g, i: (b, g, 0)),
        ],
        out_specs=pl.BlockSpec((tq, pairs_per_kv * LANES), lambda b, g, i: (b * nqb + i, g)),
        out_shape=jax.ShapeDtypeStruct((T, ATTN_WIDTH), BF16),
        scratch_shapes=[pltpu.VMEM((tq, S), F32)] * heads_per_kv,
        compiler_params=_params(3),
        name="attention",
    )(q4, kt, vt)


def _gelu(a):
    return 0.5 * a * (1.0 + lax.erf(a * (2.0 ** -0.5)))


def _route(sel_t, scores_t):
    tm = sel_t.shape[1]
    neg = -jnp.inf
    row8 = lax.broadcasted_iota(I32, (GROUP_SIZE, tm), 0)
    groups = [sel_t[g * GROUP_SIZE:(g + 1) * GROUP_SIZE, :] for g in range(N_GROUPS)]
    gscore = []
    for a in groups:
        m1 = jnp.max(a, axis=0, keepdims=True)
        i1 = jnp.min(jnp.where(a == m1, row8, GROUP_SIZE), axis=0, keepdims=True)
        m2 = jnp.max(jnp.where(row8 == i1, neg, a), axis=0, keepdims=True)
        gscore.append(m1 + m2)
    masked = []
    for g in range(N_GROUPS):
        rank = jnp.zeros((1, tm), I32)
        for g2 in range(N_GROUPS):
            if g2 == g:
                continue
            if g2 < g:
                ahead = gscore[g2] >= gscore[g]
            else:
                ahead = gscore[g2] > gscore[g]
            rank = rank + ahead.astype(I32)
        keep = rank < TOPK_GROUPS
        masked.append(jnp.where(keep, groups[g], neg))
    cur = jnp.concatenate(masked, axis=0)
    row = lax.broadcasted_iota(I32, (N_EXPERTS, tm), 0)
    wts, hits = [], []
    chosen = jnp.zeros((N_EXPERTS, tm), F32)
    for _ in range(TOP_K):
        m = jnp.max(cur, axis=0, keepdims=True)
        idx = jnp.min(jnp.where(cur == m, row, N_EXPERTS), axis=0, keepdims=True)
        hit = row == idx
        wts.append(jnp.sum(jnp.where(hit, scores_t, 0.0), axis=0, keepdims=True))
        hits.append(hit)
        chosen = jnp.where(hit, 1.0, chosen)
        cur = jnp.where(hit, neg, cur)
    return wts, hits, chosen


def _mix_kernel(z_ref, ga_ref, gb_ref, oa_ref, x_ref, g1_ref, sc2_ref, sh2_ref,
                sg_ref, sb_ref, wcat_ref, bexp_ref, wa_ref, wb_ref, wo_ref,
                l1g_ref, l1b_ref, rwh_ref, rwl_ref, rb_ref, tri_ref, etri_ref,
                x1_ref, h2_ref, loc_ref, wts_ref, n8_ref, *, alpha):
    tm = z_ref.shape[0]
    z = _gelu(z_ref[...].astype(F32))
    u = z[:, :SGU_WIDTH]
    v = _layer_norm(z[:, SGU_WIDTH:], sg_ref[...], sb_ref[...])
    lane = lax.broadcasted_iota(I32, (SGU_CHUNK, SGU_WIDTH), 1)
    wcat = wcat_ref[...]
    svs = []
    for c in range(tm // SGU_CHUNK):
        vc = v[c * SGU_CHUNK:(c + 1) * SGU_CHUNK, :]
        stacked = jnp.concatenate(
            [jnp.where(lane // SGU_GROUP_DIM == g, vc, 0.0) for g in range(SGU_GROUPS)],
            axis=0).astype(BF16)
        svs.append(_dot(wcat, stacked) + bexp_ref[...])
    sv = jnp.concatenate(svs, axis=0) if len(svs) > 1 else svs[0]
    ob = (u * sv).astype(BF16)
    ya = _dot(oa_ref[...], wa_ref[...])
    yb = _dot(ob, wb_ref[...])
    y = (jax.nn.sigmoid(ga_ref[...].astype(F32)) * ya
         + jax.nn.sigmoid(gb_ref[...].astype(F32)) * yb)
    out = _dot(y.astype(BF16), wo_ref[...])
    x1 = _layer_norm(alpha * x_ref[...] + g1_ref[0] * out, l1g_ref[...], l1b_ref[...])
    x1_ref[...] = x1
    h2 = x1 * (1.0 + sc2_ref[0]) + sh2_ref[0]
    h2_ref[...] = h2.astype(BF16)
    hh, hl = _split_bf16(h2)
    rwh = rwh_ref[...]
    logits_t = _dot_nt(rwh, hh) + _dot_nt(rwh, hl) + _dot_nt(rwl_ref[...], hh)
    scores_t = jax.nn.sigmoid(logits_t)
    sel_t = scores_t + rb_ref[:, 0:1]
    wts, hits, chosen = _route(sel_t, scores_t)
    wsum = wts[0]
    for w in wts[1:]:
        wsum = wsum + w
    inv = ROUTED_SCALE / wsum
    rank = _dot(chosen.astype(BF16), tri_ref[...])
    n = jnp.sum(chosen, axis=1, keepdims=True)
    n8 = jnp.floor((n + (SUBLANES - 1)) * (1.0 / SUBLANES)) * SUBLANES
    n8_wide = jnp.broadcast_to(n8, (N_EXPERTS, LANES))
    run_off = _dot(etri_ref[...], n8_wide.astype(BF16))
    slot = rank + run_off[:, 0:1]
    locs = [jnp.sum(jnp.where(h, slot, 0.0), axis=0, keepdims=True) for h in hits]
    loc_ref[...] = jnp.concatenate(locs, axis=0).astype(I32)
    wts_ref[...] = jnp.concatenate([w * inv for w in wts], axis=0)
    n8_ref[...] = n8_wide.astype(I32)


def _mix(rest, oa, x2, gate1, scale2, shift2, sg, sb, wcat, bexp, wa, wb, wo, l1g, l1b,
         rwh, rwl, rb, tri, etri, B, S, alpha):
    T, D = x2.shape
    tm = min(TM_ROUTE, S)
    npb = S // tm
    const = lambda i: (0, 0)
    modspec = pl.BlockSpec((1, 1, D), lambda i: (i // npb, 0, 0))
    full = lambda a: pl.BlockSpec(a.shape, const)
    return pl.pallas_call(
        functools.partial(_mix_kernel, alpha=alpha),
        grid=(T // tm,),
        in_specs=[
            pl.BlockSpec((tm, D), lambda i: (i, 0)),
            pl.BlockSpec((tm, D), lambda i: (i, 1)),
            pl.BlockSpec((tm, D), lambda i: (i, 2)),
            pl.BlockSpec((tm, ATTN_WIDTH), lambda i: (i, 0)),
            pl.BlockSpec((tm, D), lambda i: (i, 0)),
            modspec, modspec, modspec,
            full(sg), full(sb), full(wcat), full(bexp), full(wa), full(wb), full(wo),
            full(l1g), full(l1b), full(rwh), full(rwl), full(rb), full(tri), full(etri),
        ],
        out_specs=[
            pl.BlockSpec((tm, D), lambda i: (i, 0)),
            pl.BlockSpec((tm, D), lambda i: (i, 0)),
            pl.BlockSpec((TOP_K, tm), lambda i: (0, i)),
            pl.BlockSpec((TOP_K, tm), lambda i: (0, i)),
            pl.BlockSpec((None, N_EXPERTS, LANES), lambda i: (i, 0, 0)),
        ],
        out_shape=[
            jax.ShapeDtypeStruct((T, D), F32),
            jax.ShapeDtypeStruct((T, D), BF16),
            jax.ShapeDtypeStruct((TOP_K, T), I32),
            jax.ShapeDtypeStruct((TOP_K, T), F32),
            jax.ShapeDtypeStruct((T // tm, N_EXPERTS, LANES), I32),
        ],
        compiler_params=_params(1),
        name="mix",
    )(rest, rest, rest, oa, x2, gate1, scale2, shift2, sg, sb, wcat, bexp, wa, wb, wo,
      l1g, l1b, rwh, rwl, rb, tri, etri)


SMEM_ROW = 128


def _smem_table(a):
    a = a.astype(I32)
    a = jnp.concatenate([a, jnp.sum(a, axis=1, keepdims=True)], axis=1)
    return jnp.pad(a, ((0, 0), (0, SMEM_ROW - a.shape[1]))).reshape(-1)


def _wait_all_runs(n_ref, vmem_ref, hbm_ref, sem):
    total = pl.multiple_of(n_ref[N_EXPERTS], SUBLANES)
    _run_copy(vmem_ref, 0, hbm_ref, 0, total, sem).wait()


def _tile_rows(tm):
    return tm * TOP_K + N_EXPERTS * SUBLANES


def _run_copy(src_ref, s, dst_ref, d, n, sem):
    return pltpu.make_async_copy(src_ref.at[pl.ds(s, n)], dst_ref.at[pl.ds(d, n)], sem)


def _for_each_run(n_ref, fn):
    for e in range(N_EXPERTS):
        n = pl.multiple_of(n_ref[e], SUBLANES)

        @pl.when(n > 0)
        def _():
            fn(e, n)


def _dispatch_kernel(n8_ref, loff_ref, gbase_ref, n8_prev_ref, h_ref, loc_ref,
                     xs_ref, xp_ref, sems):
    i = pl.program_id(0)
    rows, tm = xp_ref.shape[1], h_ref.shape[0]
    slot = i % 2

    loc16 = loc_ref[...].astype(I16)
    hb = h_ref[...]
    half = hb.shape[1] // 2
    for r0 in range(0, rows, PERM_BLOCK):
        row = (lax.broadcasted_iota(I32, (PERM_BLOCK, tm), 0) + r0).astype(I16)
        perm = jnp.zeros((PERM_BLOCK, tm), BF16)
        one = jnp.ones((PERM_BLOCK, tm), BF16)
        for k in range(TOP_K):
            perm = jnp.where(row == loc16[k:k + 1, :], one, perm)
        xp = _dot(perm, hb)
        xp_ref[slot, r0:r0 + PERM_BLOCK, :] = (
            lax.bitcast_convert_type(xp[:, :half], U32)
            | (lax.bitcast_convert_type(xp[:, half:], U32) >> 16))

    def run_copy(e, n):
        return _run_copy(xp_ref.at[slot], pl.multiple_of(loff_ref[e], SUBLANES),
                         xs_ref, pl.multiple_of(gbase_ref[e], SUBLANES), n, sems.at[slot])

    _for_each_run(n8_ref, lambda e, n: run_copy(e, n).start())

    @pl.when(i > 0)
    def _():
        _wait_all_runs(n8_prev_ref, xp_ref.at[1 - slot], xs_ref, sems.at[1 - slot])

    @pl.when(i == pl.num_programs(0) - 1)
    def _():
        _wait_all_runs(n8_ref, xp_ref.at[slot], xs_ref, sems.at[slot])


def _dispatch(n8_flat, loff_flat, gbase_flat, h2b, loc_t, n_rows):
    T, D = h2b.shape
    tm = min(TM_ROUTE, T)
    rows = _tile_rows(tm)
    per_tile = pl.BlockSpec((SMEM_ROW,), lambda i: (i,), memory_space=pltpu.SMEM)
    prev_tile = pl.BlockSpec((SMEM_ROW,), lambda i: (jnp.maximum(i - 1, 0),),
                             memory_space=pltpu.SMEM)
    return pl.pallas_call(
        _dispatch_kernel,
        grid=(T // tm,),
        in_specs=[
            per_tile, per_tile, per_tile, prev_tile,
            pl.BlockSpec((tm, D), lambda i: (i, 0)),
            pl.BlockSpec((TOP_K, tm), lambda i: (0, i)),
        ],
        out_specs=pl.BlockSpec(memory_space=pl.ANY),
        out_shape=jax.ShapeDtypeStruct((n_rows, D // 2), U32),
        scratch_shapes=[pltpu.VMEM((2, rows, D // 2), U32), pltpu.SemaphoreType.DMA((2,))],
        compiler_params=_params(1),
        name="dispatch",
    )(n8_flat, loff_flat, gbase_flat, n8_flat, h2b, loc_t)


N_XBUF = 3
N_YBUF = 2


def _experts_kernel(gstart_ref, nch_ref, cstart_ref, crows_ref, total_ref,
                    xs_ref, wg_ref, wu_ref, wd_ref, ys_ref,
                    xbuf, ybuf, wg_s, wu_s, wd_s, sem_in, sem_out):
    e = pl.program_id(0)
    total = total_ref[0]

    def in_copy(g):
        s = g % N_XBUF
        n = pl.multiple_of(crows_ref[g], SUBLANES)
        return _run_copy(xs_ref, pl.multiple_of(cstart_ref[g], SUBLANES), xbuf.at[s], 0, n,
                         sem_in.at[s])

    def out_copy(g):
        s = g % N_YBUF
        n = pl.multiple_of(crows_ref[g], SUBLANES)
        return _run_copy(ybuf.at[s], 0, ys_ref, pl.multiple_of(cstart_ref[g], SUBLANES), n,
                         sem_out.at[s])

    def when_chunk(g, fn):
        @pl.when(jnp.logical_and(g >= 0, g < total))
        def _():
            fn(g)

    @pl.when(e == 0)
    def _():
        xbuf[...] = jnp.zeros_like(xbuf)
        for g0 in range(N_XBUF - 1):
            when_chunk(g0, lambda g: in_copy(g).start())

    wg_s[...] = wg_ref[...].astype(BF16)
    wu_s[...] = wu_ref[...].astype(BF16)
    wd_s[...] = wd_ref[...].astype(BF16)

    def chunk(c, carry):
        g = gstart_ref[e] + c
        in_copy(g).wait()
        when_chunk(g + (N_XBUF - 1), lambda g2: in_copy(g2).start())
        xb = _unpack_pairs(xbuf[g % N_XBUF])
        gate = _dot(xb, wg_s[...])
        up = _dot(xb, wu_s[...])
        hb = (gate * jax.nn.sigmoid(gate) * up).astype(BF16)
        y = _pack_pairs(_dot(hb, wd_s[...]))
        when_chunk(g - N_YBUF, lambda g2: out_copy(g2).wait())
        ybuf[g % N_YBUF] = y
        out_copy(g).start()
        return carry
    lax.fori_loop(0, nch_ref[e], chunk, 0)

    @pl.when(e == pl.num_programs(0) - 1)
    def _():
        for back in range(N_YBUF, 0, -1):
            when_chunk(total - back, lambda g: out_copy(g).wait())


def _experts(gstart, nch, cstart, crows, total, xs, w_gate, w_up, w_down, layer):
    _, E, D, FF = w_gate.shape
    wspec = lambda a, b: pl.BlockSpec((None, None, a, b), lambda e, *_: (layer, e, 0, 0))
    grid_spec = pltpu.PrefetchScalarGridSpec(
        num_scalar_prefetch=5,
        grid=(E,),
        in_specs=[pl.BlockSpec(memory_space=pl.ANY), wspec(D, FF), wspec(D, FF), wspec(FF, D)],
        out_specs=pl.BlockSpec(memory_space=pl.ANY),
        scratch_shapes=[
            pltpu.VMEM((N_XBUF, CH_EXP, D // 2), U32),
            pltpu.VMEM((N_YBUF, CH_EXP, D // 2), U32),
            pltpu.VMEM((D, FF), BF16),
            pltpu.VMEM((D, FF), BF16),
            pltpu.VMEM((FF, D), BF16),
            pltpu.SemaphoreType.DMA((N_XBUF,)),
            pltpu.SemaphoreType.DMA((N_YBUF,)),
        ],
    )
    return pl.pallas_call(
        _experts_kernel,
        grid_spec=grid_spec,
        out_shape=jax.ShapeDtypeStruct(xs.shape, U32),
        compiler_params=_params(1),
        name="experts",
    )(gstart, nch, cstart, crows, total, xs, w_gate, w_up, w_down)


def _combine_kernel(n8_ref, loff_ref, gbase_ref, n8_nxt_ref, loff_nxt_ref, gbase_nxt_ref,
                    ys_ref, loc_ref, w_ref, h_ref, x1_ref, g2_ref,
                    wsg_ref, wsu_ref, wsd_ref, l2g_ref, l2b_ref, o_ref, yp_ref, sems, *, alpha):
    i = pl.program_id(0)
    rows, tm = yp_ref.shape[1], h_ref.shape[0]
    slot = i % 2

    def fetch(lo_ref, gb_ref, s):
        def copy(e, n):
            return _run_copy(ys_ref, pl.multiple_of(gb_ref[e], SUBLANES),
                             yp_ref.at[s], pl.multiple_of(lo_ref[e], SUBLANES), n, sems.at[s])
        return copy

    @pl.when(i == 0)
    def _():
        yp_ref[...] = jnp.zeros_like(yp_ref)
        cur = fetch(loff_ref, gbase_ref, slot)
        _for_each_run(n8_ref, lambda e, n: cur(e, n).start())

    @pl.when(i + 1 < pl.num_programs(0))
    def _():
        nxt = fetch(loff_nxt_ref, gbase_nxt_ref, 1 - slot)
        _for_each_run(n8_nxt_ref, lambda e, n: nxt(e, n).start())

    hb = h_ref[...]
    g = _dot(hb, wsg_ref[...])
    u = _dot(hb, wsu_ref[...])
    y = _dot((g * jax.nn.sigmoid(g) * u).astype(BF16), wsd_ref[...])

    loc = loc_ref[...]
    w = w_ref[...]
    pws = []
    for r0 in range(0, rows, PERM_BLOCK):
        lane = (lax.broadcasted_iota(I32, (tm, PERM_BLOCK), 1) + r0).astype(I16)
        pw = jnp.zeros((tm, PERM_BLOCK), BF16)
        for k in range(TOP_K):
            lk = jnp.broadcast_to(loc[:, k:k + 1], (tm, PERM_BLOCK)).astype(I16)
            wk = jnp.broadcast_to(w[:, k:k + 1], (tm, PERM_BLOCK)).astype(BF16)
            pw = jnp.where(lane == lk, wk, pw)
        pws.append(pw)

    _wait_all_runs(n8_ref, yp_ref.at[slot], ys_ref, sems.at[slot])

    for j, r0 in enumerate(range(0, rows, PERM_BLOCK)):
        yp = _unpack_pairs(yp_ref[slot, r0:r0 + PERM_BLOCK, :])
        y = y + _dot(pws[j], yp)
    o_ref[...] = _layer_norm(alpha * x1_ref[...] + g2_ref[0] * y, l2g_ref[...], l2b_ref[...])


def _combine(n8_flat, loff_flat, gbase_flat, ys, loc_tk, w_tk, h2b, x1, gate2, wsg, wsu, wsd,
             l2g, l2b, B, S, alpha):
    T, D = h2b.shape
    tm = min(TM_ROUTE, S)
    npb = S // tm
    rows = _tile_rows(tm)
    const = lambda i: (0, 0)
    full = lambda a: pl.BlockSpec(a.shape, const)
    n_tiles = T // tm
    per_tile = pl.BlockSpec((SMEM_ROW,), lambda i: (i,), memory_space=pltpu.SMEM)
    next_tile = pl.BlockSpec((SMEM_ROW,), lambda i: (jnp.minimum(i + 1, n_tiles - 1),),
                             memory_space=pltpu.SMEM)
    return pl.pallas_call(
        functools.partial(_combine_kernel, alpha=alpha),
        grid=(n_tiles,),
        in_specs=[
            per_tile, per_tile, per_tile, next_tile, next_tile, next_tile,
            pl.BlockSpec(memory_space=pl.ANY),
            pl.BlockSpec((tm, TOP_K), lambda i: (i, 0)),
            pl.BlockSpec((tm, TOP_K), lambda i: (i, 0)),
            pl.BlockSpec((tm, D), lambda i: (i, 0)),
            pl.BlockSpec((tm, D), lambda i: (i, 0)),
            pl.BlockSpec((1, 1, D), lambda i: (i // npb, 0, 0)),
            full(wsg), full(wsu), full(wsd), full(l2g), full(l2b),
        ],
        out_specs=pl.BlockSpec((tm, D), lambda i: (i, 0)),
        out_shape=jax.ShapeDtypeStruct((T, D), F32),
        scratch_shapes=[pltpu.VMEM((2, rows, D // 2), U32), pltpu.SemaphoreType.DMA((2,))],
        compiler_params=_params(1),
        name="combine",
    )(n8_flat, loff_flat, gbase_flat, n8_flat, loff_flat, gbase_flat,
      ys, loc_tk, w_tk, h2b, x1, gate2, wsg, wsu, wsd, l2g, l2b)


def _rope_tables(S):
    rows = S // GRID_W
    t = jnp.arange(S)
    row = (t // GRID_W - rows // 2).astype(F32)
    col = (t % GRID_W - GRID_W // 2).astype(F32)
    n_pairs = HEAD_DIM // 4
    inv_freq = ROPE_THETA ** (-jnp.arange(n_pairs, dtype=F32) / n_pairs)
    ang = jnp.concatenate([row[:, None] * inv_freq, col[:, None] * inv_freq], -1)
    cos, sin = jnp.cos(ang), jnp.sin(ang)
    cos_i = jnp.repeat(cos, 2, axis=-1)
    sin_i = jnp.stack([-sin, sin], axis=-1).reshape(S, HEAD_DIM)
    reps = LANES // HEAD_DIM
    return jnp.tile(cos_i, (1, reps)), jnp.tile(sin_i, (1, reps))


def kernel(x, c, w_ada, b_ada, w_in, q_scale, k_scale, sgu_ln_g, sgu_ln_b, w_spatial, b_spatial,
           w_branch_a, w_branch_b, w_out, ln1_g, ln1_b, router_w, router_bias, w_gate, w_up, w_down,
           ws_gate, ws_up, ws_down, ln2_g, ln2_b):
    B, S, D = x.shape
    T = B * S
    L = w_ada.shape[0]
    alpha = (2 * L) ** 0.25
    tm = min(TM_ROUTE, S)
    n_tiles = T // tm
    n_rows = T * TOP_K + n_tiles * N_EXPERTS * SUBLANES
    max_chunks = n_rows // CH_EXP + N_EXPERTS

    cos2, sin2 = _rope_tables(S)
    mod = _modulation(c, w_ada, b_ada)
    hd = jnp.arange(ATTN_WIDTH) // HEAD_DIM
    gsum = (hd[:, None] == hd[None, :]).astype(BF16)
    tri = (jnp.arange(tm)[:, None] < jnp.arange(tm)[None, :]).astype(BF16)
    etri = (jnp.arange(N_EXPERTS)[None, :] < jnp.arange(N_EXPERTS)[:, None]).astype(BF16)

    x2 = x.reshape(T, D)
    for l in range(L):
        m6 = mod[l].reshape(B, 1, 6, D)
        shift1, scale1, gate1, shift2, scale2, gate2 = [m6[:, :, j, :] for j in range(6)]
        qs = jnp.tile(q_scale[l], N_Q_HEADS).reshape(1, ATTN_WIDTH)
        ks = jnp.tile(k_scale[l], N_KV_HEADS).reshape(1, KV_WIDTH)
        q4, kt, vt, rest = _inproj(x2, scale1, shift1, w_in[l].astype(BF16), cos2, sin2, qs, ks,
                                   gsum, B, S)
        oa = _attention(q4, kt, vt, B, S)

        wcat = jnp.transpose(w_spatial[l], (1, 0, 2)).reshape(SGU_CHUNK, SGU_GROUPS * SGU_CHUNK)
        bexp = jnp.repeat(b_spatial[l].T, SGU_GROUP_DIM, axis=1)
        rw_t = router_w[l].T
        rwh = rw_t.astype(BF16)
        rwl = (rw_t - rwh.astype(F32)).astype(BF16)
        rb = jnp.broadcast_to(router_bias[l].astype(F32)[:, None], (N_EXPERTS, LANES))
        x1, h2b, loc_t, wts_t, n8_out = _mix(
            rest, oa, x2, gate1, scale2, shift2,
            sgu_ln_g[l].reshape(1, -1), sgu_ln_b[l].reshape(1, -1), wcat.astype(BF16), bexp,
            w_branch_a[l].astype(BF16), w_branch_b[l].astype(BF16), w_out[l].astype(BF16),
            ln1_g[l].reshape(1, -1), ln1_b[l].reshape(1, -1), rwh, rwl, rb, tri, etri, B, S, alpha)

        n8 = n8_out[:, :, 0]
        tot = jnp.sum(n8, axis=0)
        pstarts = jnp.cumsum(tot) - tot
        gbase = pstarts[None, :] + jnp.cumsum(n8, axis=0) - n8
        loff = jnp.cumsum(n8, axis=1) - n8
        n8_flat, loff_flat, gbase_flat = [_smem_table(a) for a in (n8, loff, gbase)]
        nch = (tot + CH_EXP - 1) // CH_EXP
        gend = jnp.cumsum(nch)
        gstart = gend - nch
        gidx = jnp.arange(max_chunks, dtype=I32)
        owner = (gend[None, :] <= gidx[:, None]).astype(I32).sum(axis=1)
        onehot = (owner[:, None] == jnp.arange(N_EXPERTS)[None, :]).astype(I32)
        pick = lambda v: jnp.sum(onehot * v[None, :], axis=1)
        within = (gidx - pick(gstart)) * CH_EXP
        crows = jnp.clip(pick(tot) - within, 0, CH_EXP).astype(I32)
        cstart = (pick(pstarts) + within).astype(I32)
        total = gend[-1].astype(I32).reshape(1)

        xs = _dispatch(n8_flat, loff_flat, gbase_flat, h2b, loc_t, n_rows)
        ys = _experts(gstart.astype(I32), nch.astype(I32), cstart, crows, total, xs,
                      w_gate, w_up, w_down, l)
        x2 = _combine(n8_flat, loff_flat, gbase_flat, ys, loc_t.T, wts_t.T, h2b, x1, gate2,
                      ws_gate[l].astype(BF16), ws_up[l].astype(BF16), ws_down[l].astype(BF16),
                      ln2_g[l].reshape(1, -1), ln2_b[l].reshape(1, -1), B, S, alpha)
    return x2.reshape(B, S, D)
```

```python
import functools

import jax
import jax.numpy as jnp
from jax import lax
from jax.experimental import pallas as pl
from jax.experimental.pallas import tpu as pltpu

F32 = jnp.float32
BF16 = jnp.bfloat16
I32 = jnp.int32
I16 = jnp.int16
U32 = jnp.uint32

GRID_W = 64
N_Q_HEADS = 8
N_KV_HEADS = 2
HEAD_DIM = 64
ROPE_THETA = 10000.0
ATTN_WIDTH = N_Q_HEADS * HEAD_DIM
KV_WIDTH = N_KV_HEADS * HEAD_DIM
SGU_GROUPS = 8
SGU_WIDTH = 512
SGU_GROUP_DIM = SGU_WIDTH // SGU_GROUPS
SGU_CHUNK = 128
N_EXPERTS = 64
TOP_K = 8
N_GROUPS = 8
TOPK_GROUPS = 4
GROUP_SIZE = N_EXPERTS // N_GROUPS
ROUTED_SCALE = 2.5
LN_EPS = 1e-5
RMS_EPS = 1e-6

LANES = 128
SUBLANES = 8
VMEM_LIMIT = 56 * 1024 * 1024

TM_IN = 256
TQ = 256
KV_CHUNK = 512
LOG2_E = 1.4426950408889634
TM_ROUTE = 256
TM_MIX = 512
CH_EXP = 512
PERM_BLOCK = 512


def _params(n_axes):
    return pltpu.CompilerParams(
        dimension_semantics=("arbitrary",) * n_axes, vmem_limit_bytes=VMEM_LIMIT)


def _dot(a, b):
    return jnp.dot(a, b, preferred_element_type=F32)


def _dot_nt(a, b):
    return lax.dot_general(a, b, (((1,), (1,)), ((), ())), preferred_element_type=F32)


def _split_bf16(a):
    hi = a.astype(BF16)
    lo = (a - hi.astype(F32)).astype(BF16)
    return hi, lo


def _layer_norm(r, g, b):
    mu = jnp.mean(r, axis=-1, keepdims=True)
    d = r - mu
    var = jnp.mean(d * d, axis=-1, keepdims=True)
    return d * lax.rsqrt(var + LN_EPS) * g + b


def _pack_pairs(y):
    n = y.shape[1] // 2
    hi = lax.bitcast_convert_type(y[:, :n].astype(BF16).astype(F32), U32)
    lo = lax.bitcast_convert_type(y[:, n:].astype(BF16).astype(F32), U32)
    return hi | (lo >> 16)


def _unpack_pairs(w):
    hi = lax.bitcast_convert_type(w & jnp.uint32(0xFFFF0000), F32).astype(BF16)
    lo = lax.bitcast_convert_type(w << 16, F32).astype(BF16)
    return jnp.concatenate([hi, lo], axis=1)


def _mod_kernel(c_ref, w_ref, b_ref, o_ref):
    c = c_ref[...]
    cond = c * jax.nn.sigmoid(c)
    o_ref[...] = _dot(cond, w_ref[...]) + b_ref[...]


def _modulation(c, w_ada, b_ada):
    L, D, D6 = w_ada.shape
    B = c.shape[0]
    nj = D6 // D
    return pl.pallas_call(
        _mod_kernel,
        grid=(L, nj),
        in_specs=[
            pl.BlockSpec((B, D), lambda l, j: (0, 0)),
            pl.BlockSpec((None, D, D), lambda l, j: (l, 0, j)),
            pl.BlockSpec((None, 1, D), lambda l, j: (l, 0, j)),
        ],
        out_specs=pl.BlockSpec((None, B, D), lambda l, j: (l, 0, j)),
        out_shape=jax.ShapeDtypeStruct((L, B, D6), F32),
        compiler_params=_params(2),
        name="modulation",
    )(c, w_ada, b_ada.reshape(L, 1, D6))


def _swap_pairs(a):
    n = a.shape[1]
    lane = lax.broadcasted_iota(I32, a.shape, 1)
    nxt = pltpu.roll(a, n - 1, axis=1)
    prv = pltpu.roll(a, 1, axis=1)
    return jnp.where((lane & 1) == 0, nxt, prv)


def _head_rms(p, gsum_ref, width):
    sq = p * p
    hi, lo = _split_bf16(sq)
    g = gsum_ref[0:width, 0:width]
    ss = _dot(hi, g) + _dot(lo, g)
    return lax.rsqrt(ss * (1.0 / HEAD_DIM) + RMS_EPS)


def _inproj_kernel(x_ref, sc_ref, sh_ref, w_ref, cos_ref, sin_ref, qs_ref, ks_ref, gsum_ref,
                   q_ref, kt_ref, vt_ref, r_ref):
    x = x_ref[...]
    h = (x * (1.0 + sc_ref[0]) + sh_ref[0]).astype(BF16)
    cos2 = cos_ref[...]
    sin2 = sin_ref[...]
    pq = _dot(h, w_ref[:, 0:ATTN_WIDTH])
    qn = pq * _head_rms(pq, gsum_ref, ATTN_WIDTH) * qs_ref[...]
    cos8 = jnp.concatenate([cos2] * (ATTN_WIDTH // LANES), axis=1)
    sin8 = jnp.concatenate([sin2] * (ATTN_WIDTH // LANES), axis=1)
    qr = (qn * cos8 + _swap_pairs(qn) * sin8) * (HEAD_DIM ** -0.5 * LOG2_E)
    for p in range(ATTN_WIDTH // LANES):
        q_ref[p] = qr[:, p * LANES:(p + 1) * LANES].astype(BF16)
    k0 = ATTN_WIDTH
    pk = _dot(h, w_ref[:, k0:k0 + KV_WIDTH])
    kn = pk * _head_rms(pk, gsum_ref, KV_WIDTH) * ks_ref[...]
    kr = kn * cos2 + _swap_pairs(kn) * sin2
    kt_ref[...] = kr.T.astype(BF16)
    v0 = k0 + KV_WIDTH
    pv = _dot(h, w_ref[:, v0:v0 + KV_WIDTH])
    vt_ref[...] = pv.T.astype(BF16)
    r0 = v0 + KV_WIDTH
    n_rest = r_ref.shape[1]
    step = 1024
    for j in range(n_rest // step):
        r_ref[:, j * step:(j + 1) * step] = _dot(
            h, w_ref[:, r0 + j * step:r0 + (j + 1) * step]).astype(BF16)


def _inproj(x2, scale1, shift1, w_in_b, cos2, sin2, qs, ks, gsum, B, S):
    T, D = x2.shape
    tm = min(TM_IN, S)
    npb = S // tm
    n_rest = w_in_b.shape[1] - ATTN_WIDTH - 2 * KV_WIDTH
    nq = ATTN_WIDTH // LANES
    const = lambda i: (0, 0)
    return pl.pallas_call(
        _inproj_kernel,
        grid=(T // tm,),
        in_specs=[
            pl.BlockSpec((tm, D), lambda i: (i, 0)),
            pl.BlockSpec((1, 1, D), lambda i: (i // npb, 0, 0)),
            pl.BlockSpec((1, 1, D), lambda i: (i // npb, 0, 0)),
            pl.BlockSpec(w_in_b.shape, const),
            pl.BlockSpec((tm, LANES), lambda i: (i % npb, 0)),
            pl.BlockSpec((tm, LANES), lambda i: (i % npb, 0)),
            pl.BlockSpec((1, ATTN_WIDTH), const),
            pl.BlockSpec((1, KV_WIDTH), const),
            pl.BlockSpec(gsum.shape, const),
        ],
        out_specs=[
            pl.BlockSpec((nq, tm, LANES), lambda i: (0, i, 0)),
            pl.BlockSpec((None, KV_WIDTH, tm), lambda i: (i // npb, 0, i % npb)),
            pl.BlockSpec((None, KV_WIDTH, tm), lambda i: (i // npb, 0, i % npb)),
            pl.BlockSpec((tm, n_rest), lambda i: (i, 0)),
        ],
        out_shape=[
            jax.ShapeDtypeStruct((nq, T, LANES), BF16),
            jax.ShapeDtypeStruct((B, KV_WIDTH, S), BF16),
            jax.ShapeDtypeStruct((B, KV_WIDTH, S), BF16),
            jax.ShapeDtypeStruct((T, n_rest), BF16),
        ],
        compiler_params=_params(1),
        name="inproj",
    )(x2, scale1, shift1, w_in_b, cos2, sin2, qs, ks, gsum)


def _attn_kernel(q_ref, kt_ref, vt_ref, o_ref, *s_refs):
    kt = kt_ref[...]
    vt = vt_ref[...]
    tq, S = s_refs[0].shape
    kc = min(KV_CHUNK, S)
    heads = []
    for pp in range(q_ref.shape[0]):
        q2 = q_ref[pp]
        for hh in range(LANES // HEAD_DIM):
            heads.append(q2[:, hh * HEAD_DIM:(hh + 1) * HEAD_DIM])
    s_refs[0][...] = _dot(heads[0], kt)
    outs = []
    for h, s_ref in enumerate(s_refs):
        if h + 1 < len(heads):
            s_refs[h + 1][...] = _dot(heads[h + 1], kt)
        m = jnp.max(s_ref[...], axis=-1, keepdims=True)
        l = jnp.zeros_like(m)
        acc = jnp.zeros((tq, HEAD_DIM), F32)
        for c in range(S // kc):
            p = jnp.exp2(s_ref[:, c * kc:(c + 1) * kc] - m)
            l = l + jnp.sum(p, axis=-1, keepdims=True)
            acc = acc + _dot_nt(p.astype(BF16), vt[:, c * kc:(c + 1) * kc])
        outs.append(acc / l)
    o_ref[...] = jnp.concatenate(outs, axis=1).astype(BF16)


def _attention(q4, kt, vt, B, S):
    nq, T, _ = q4.shape
    tq = min(TQ, S)
    nqb = S // tq
    pairs_per_kv = nq // N_KV_HEADS
    heads_per_kv = N_Q_HEADS // N_KV_HEADS
    return pl.pallas_call(
        _attn_kernel,
        grid=(B, N_KV_HEADS, nqb),
        in_specs=[
            pl.BlockSpec((pairs_per_kv, tq, LANES), lambda b, g, i: (g, b * nqb + i, 0)),
            pl.BlockSpec((None, HEAD_DIM, S), lambda b, g, i: (b, g, 0)),
            pl.BlockSpec((None, HEAD_DIM, S), lambda b, g, i: (b, g, 0)),
        ],
        out_specs=pl.BlockSpec((tq, pairs_per_kv * LANES), lambda b, g, i: (b * nqb + i, g)),
        out_shape=jax.ShapeDtypeStruct((T, ATTN_WIDTH), BF16),
        scratch_shapes=[pltpu.VMEM((tq, S), F32)] * heads_per_kv,
        compiler_params=_params(3),
        name="attention",
    )(q4, kt, vt)


def _gelu(a):
    return 0.5 * a * (1.0 + lax.erf(a * (2.0 ** -0.5)))


def _route(sel_t, scores_t):
    tm = sel_t.shape[1]
    neg = -jnp.inf
    row8 = lax.broadcasted_iota(I32, (GROUP_SIZE, tm), 0)
    groups = [sel_t[g * GROUP_SIZE:(g + 1) * GROUP_SIZE, :] for g in range(N_GROUPS)]
    gscore = []
    for a in groups:
        m1 = jnp.max(a, axis=0, keepdims=True)
        i1 = jnp.min(jnp.where(a == m1, row8, GROUP_SIZE), axis=0, keepdims=True)
        m2 = jnp.max(jnp.where(row8 == i1, neg, a), axis=0, keepdims=True)
        gscore.append(m1 + m2)
    masked = []
    for g in range(N_GROUPS):
        rank = jnp.zeros((1, tm), I32)
        for g2 in range(N_GROUPS):
            if g2 == g:
                continue
            if g2 < g:
                ahead = gscore[g2] >= gscore[g]
            else:
                ahead = gscore[g2] > gscore[g]
            rank = rank + ahead.astype(I32)
        keep = rank < TOPK_GROUPS
        masked.append(jnp.where(keep, groups[g], neg))
    cur = jnp.concatenate(masked, axis=0)
    row = lax.broadcasted_iota(I32, (N_EXPERTS, tm), 0)
    wts, hits = [], []
    chosen = jnp.zeros((N_EXPERTS, tm), F32)
    for _ in range(TOP_K):
        m = jnp.max(cur, axis=0, keepdims=True)
        idx = jnp.min(jnp.where(cur == m, row, N_EXPERTS), axis=0, keepdims=True)
        hit = row == idx
        wts.append(jnp.sum(jnp.where(hit, scores_t, 0.0), axis=0, keepdims=True))
        hits.append(hit)
        chosen = jnp.where(hit, 1.0, chosen)
        cur = jnp.where(hit, neg, cur)
    return wts, hits, chosen


def _mix_kernel(z_ref, ga_ref, gb_ref, oa_ref, x_ref, g1_ref, sc2_ref, sh2_ref,
                sg_ref, sb_ref, wcat_ref, bexp_ref, wa_ref, wb_ref, wo_ref,
                l1g_ref, l1b_ref, rwh_ref, rwl_ref, rb_ref, tri_ref, etri_ref,
                x1_ref, h2_ref, loc_ref, wts_ref, n8_ref, *, alpha):
    row_slices = [slice(r0, r0 + TM_ROUTE) for r0 in range(0, z_ref.shape[0], TM_ROUTE)]
    h2s = [_mix_dense(rs, z_ref, ga_ref, gb_ref, oa_ref, x_ref, g1_ref, sc2_ref, sh2_ref,
                      sg_ref, sb_ref, wcat_ref, bexp_ref, wa_ref, wb_ref, wo_ref,
                      l1g_ref, l1b_ref, x1_ref, h2_ref, alpha) for rs in row_slices]
    for sub, rs in enumerate(row_slices):
        _mix_route(sub, rs, h2s[sub], rwh_ref, rwl_ref, rb_ref, tri_ref, etri_ref,
                   loc_ref, wts_ref, n8_ref)


def _mix_dense(rs, z_ref, ga_ref, gb_ref, oa_ref, x_ref, g1_ref, sc2_ref, sh2_ref,
               sg_ref, sb_ref, wcat_ref, bexp_ref, wa_ref, wb_ref, wo_ref,
               l1g_ref, l1b_ref, x1_ref, h2_ref, alpha):
    tm = TM_ROUTE
    z = _gelu(z_ref[rs, :].astype(F32))
    u = z[:, :SGU_WIDTH]
    v = _layer_norm(z[:, SGU_WIDTH:], sg_ref[...], sb_ref[...])
    lane = lax.broadcasted_iota(I32, (SGU_CHUNK, SGU_WIDTH), 1)
    wcat = wcat_ref[...]
    svs = []
    for c in range(tm // SGU_CHUNK):
        vc = v[c * SGU_CHUNK:(c + 1) * SGU_CHUNK, :]
        stacked = jnp.concatenate(
            [jnp.where(lane // SGU_GROUP_DIM == g, vc, 0.0) for g in range(SGU_GROUPS)],
            axis=0).astype(BF16)
        svs.append(_dot(wcat, stacked) + bexp_ref[...])
    sv = jnp.concatenate(svs, axis=0) if len(svs) > 1 else svs[0]
    ob = (u * sv).astype(BF16)
    ya = _dot(oa_ref[rs, :], wa_ref[...])
    yb = _dot(ob, wb_ref[...])
    y = (jax.nn.sigmoid(ga_ref[rs, :].astype(F32)) * ya
         + jax.nn.sigmoid(gb_ref[rs, :].astype(F32)) * yb)
    out = _dot(y.astype(BF16), wo_ref[...])
    x1 = _layer_norm(alpha * x_ref[rs, :] + g1_ref[0] * out, l1g_ref[...], l1b_ref[...])
    x1_ref[rs, :] = x1
    h2 = x1 * (1.0 + sc2_ref[0]) + sh2_ref[0]
    h2_ref[rs, :] = h2.astype(BF16)
    return h2


def _mix_route(sub, rs, h2, rwh_ref, rwl_ref, rb_ref, tri_ref, etri_ref, loc_ref, wts_ref, n8_ref):
    hh, hl = _split_bf16(h2)
    rwh = rwh_ref[...]
    logits_t = _dot_nt(rwh, hh) + _dot_nt(rwh, hl) + _dot_nt(rwl_ref[...], hh)
    scores_t = jax.nn.sigmoid(logits_t)
    sel_t = scores_t + rb_ref[:, 0:1]
    wts, hits, chosen = _route(sel_t, scores_t)
    wsum = wts[0]
    for w in wts[1:]:
        wsum = wsum + w
    inv = ROUTED_SCALE / wsum
    rank = _dot(chosen.astype(BF16), tri_ref[...])
    n = jnp.sum(chosen, axis=1, keepdims=True)
    n8 = jnp.floor((n + (SUBLANES - 1)) * (1.0 / SUBLANES)) * SUBLANES
    n8_wide = jnp.broadcast_to(n8, (N_EXPERTS, LANES))
    run_off = _dot(etri_ref[...], n8_wide.astype(BF16))
    slot = rank + run_off[:, 0:1]
    locs = [jnp.sum(jnp.where(h, slot, 0.0), axis=0, keepdims=True) for h in hits]
    loc_ref[:, rs] = jnp.concatenate(locs, axis=0).astype(I32)
    wts_ref[:, rs] = jnp.concatenate([w * inv for w in wts], axis=0)
    n8_ref[sub] = n8_wide.astype(I32)


def _mix(rest, oa, x2, gate1, scale2, shift2, sg, sb, wcat, bexp, wa, wb, wo, l1g, l1b,
         rwh, rwl, rb, tri, etri, B, S, alpha):
    T, D = x2.shape
    tm = min(TM_MIX, S)
    nsub = tm // TM_ROUTE
    npb = S // tm
    const = lambda i: (0, 0)
    modspec = pl.BlockSpec((1, 1, D), lambda i: (i // npb, 0, 0))
    full = lambda a: pl.BlockSpec(a.shape, const)
    return pl.pallas_call(
        functools.partial(_mix_kernel, alpha=alpha),
        grid=(T // tm,),
        in_specs=[
            pl.BlockSpec((tm, D), lambda i: (i, 0)),
            pl.BlockSpec((tm, D), lambda i: (i, 1)),
            pl.BlockSpec((tm, D), lambda i: (i, 2)),
            pl.BlockSpec((tm, ATTN_WIDTH), lambda i: (i, 0)),
            pl.BlockSpec((tm, D), lambda i: (i, 0)),
            modspec, modspec, modspec,
            full(sg), full(sb), full(wcat), full(bexp), full(wa), full(wb), full(wo),
            full(l1g), full(l1b), full(rwh), full(rwl), full(rb), full(tri), full(etri),
        ],
        out_specs=[
            pl.BlockSpec((tm, D), lambda i: (i, 0)),
            pl.BlockSpec((tm, D), lambda i: (i, 0)),
            pl.BlockSpec((TOP_K, tm), lambda i: (0, i)),
            pl.BlockSpec((TOP_K, tm), lambda i: (0, i)),
            pl.BlockSpec((nsub, N_EXPERTS, LANES), lambda i: (i, 0, 0)),
        ],
        out_shape=[
            jax.ShapeDtypeStruct((T, D), F32),
            jax.ShapeDtypeStruct((T, D), BF16),
            jax.ShapeDtypeStruct((TOP_K, T), I32),
            jax.ShapeDtypeStruct((TOP_K, T), F32),
            jax.ShapeDtypeStruct((T // TM_ROUTE, N_EXPERTS, LANES), I32),
        ],
        compiler_params=_params(1),
        name="mix",
    )(rest, rest, rest, oa, x2, gate1, scale2, shift2, sg, sb, wcat, bexp, wa, wb, wo,
      l1g, l1b, rwh, rwl, rb, tri, etri)


SMEM_ROW = 128


def _smem_table(a):
    a = a.astype(I32)
    a = jnp.concatenate([a, jnp.sum(a, axis=1, keepdims=True)], axis=1)
    return jnp.pad(a, ((0, 0), (0, SMEM_ROW - a.shape[1]))).reshape(-1)


def _wait_all_runs(n_ref, vmem_ref, hbm_ref, sem):
    total = pl.multiple_of(n_ref[N_EXPERTS], SUBLANES)
    _run_copy(vmem_ref, 0, hbm_ref, 0, total, sem).wait()


def _tile_rows(tm):
    return tm * TOP_K + N_EXPERTS * SUBLANES


def _run_copy(src_ref, s, dst_ref, d, n, sem):
    return pltpu.make_async_copy(src_ref.at[pl.ds(s, n)], dst_ref.at[pl.ds(d, n)], sem)


def _for_each_run(n_ref, fn):
    for e in range(N_EXPERTS):
        n = pl.multiple_of(n_ref[e], SUBLANES)

        @pl.when(n > 0)
        def _():
            fn(e, n)


def _dispatch_kernel(n8_ref, loff_ref, gbase_ref, n8_prev_ref, h_ref, loc_ref,
                     xs_ref, xp_ref, sems):
    i = pl.program_id(0)
    rows, tm = xp_ref.shape[1], h_ref.shape[0]
    slot = i % 2

    loc16 = loc_ref[...].astype(I16)
    hb = h_ref[...]
    half = hb.shape[1] // 2
    for r0 in range(0, rows, PERM_BLOCK):
        row = (lax.broadcasted_iota(I32, (PERM_BLOCK, tm), 0) + r0).astype(I16)
        perm = jnp.zeros((PERM_BLOCK, tm), BF16)
        one = jnp.ones((PERM_BLOCK, tm), BF16)
        for k in range(TOP_K):
            perm = jnp.where(row == loc16[k:k + 1, :], one, perm)
        xp = _dot(perm, hb)
        xp_ref[slot, r0:r0 + PERM_BLOCK, :] = (
            lax.bitcast_convert_type(xp[:, :half], U32)
            | (lax.bitcast_convert_type(xp[:, half:], U32) >> 16))

    def run_copy(e, n):
        return _run_copy(xp_ref.at[slot], pl.multiple_of(loff_ref[e], SUBLANES),
                         xs_ref, pl.multiple_of(gbase_ref[e], SUBLANES), n, sems.at[slot])

    _for_each_run(n8_ref, lambda e, n: run_copy(e, n).start())

    @pl.when(i > 0)
    def _():
        _wait_all_runs(n8_prev_ref, xp_ref.at[1 - slot], xs_ref, sems.at[1 - slot])

    @pl.when(i == pl.num_programs(0) - 1)
    def _():
        _wait_all_runs(n8_ref, xp_ref.at[slot], xs_ref, sems.at[slot])


def _dispatch(n8_flat, loff_flat, gbase_flat, h2b, loc_t, n_rows):
    T, D = h2b.shape
    tm = min(TM_ROUTE, T)
    rows = _tile_rows(tm)
    per_tile = pl.BlockSpec((SMEM_ROW,), lambda i: (i,), memory_space=pltpu.SMEM)
    prev_tile = pl.BlockSpec((SMEM_ROW,), lambda i: (jnp.maximum(i - 1, 0),),
                             memory_space=pltpu.SMEM)
    return pl.pallas_call(
        _dispatch_kernel,
        grid=(T // tm,),
        in_specs=[
            per_tile, per_tile, per_tile, prev_tile,
            pl.BlockSpec((tm, D), lambda i: (i, 0)),
            pl.BlockSpec((TOP_K, tm), lambda i: (0, i)),
        ],
        out_specs=pl.BlockSpec(memory_space=pl.ANY),
        out_shape=jax.ShapeDtypeStruct((n_rows, D // 2), U32),
        scratch_shapes=[pltpu.VMEM((2, rows, D // 2), U32), pltpu.SemaphoreType.DMA((2,))],
        compiler_params=_params(1),
        name="dispatch",
    )(n8_flat, loff_flat, gbase_flat, n8_flat, h2b, loc_t)


N_XBUF = 3
N_YBUF = 2


def _experts_kernel(gstart_ref, nch_ref, cstart_ref, crows_ref, total_ref,
                    xs_ref, wg_ref, wu_ref, wd_ref, ys_ref,
                    xbuf, ybuf, wg_s, wu_s, wd_s, sem_in, sem_out):
    e = pl.program_id(0)
    total = total_ref[0]

    def in_copy(g):
        s = g % N_XBUF
        n = pl.multiple_of(crows_ref[g], SUBLANES)
        return _run_copy(xs_ref, pl.multiple_of(cstart_ref[g], SUBLANES), xbuf.at[s], 0, n,
                         sem_in.at[s])

    def out_copy(g):
        s = g % N_YBUF
        n = pl.multiple_of(crows_ref[g], SUBLANES)
        return _run_copy(ybuf.at[s], 0, ys_ref, pl.multiple_of(cstart_ref[g], SUBLANES), n,
                         sem_out.at[s])

    def when_chunk(g, fn):
        @pl.when(jnp.logical_and(g >= 0, g < total))
        def _():
            fn(g)

    @pl.when(e == 0)
    def _():
        xbuf[...] = jnp.zeros_like(xbuf)
        for g0 in range(N_XBUF - 1):
            when_chunk(g0, lambda g: in_copy(g).start())

    wg_s[...] = wg_ref[...].astype(BF16)
    wu_s[...] = wu_ref[...].astype(BF16)
    wd_s[...] = wd_ref[...].astype(BF16)

    def chunk(c, carry):
        g = gstart_ref[e] + c
        in_copy(g).wait()
        when_chunk(g + (N_XBUF - 1), lambda g2: in_copy(g2).start())
        xb = _unpack_pairs(xbuf[g % N_XBUF])
        gate = _dot(xb, wg_s[...])
        up = _dot(xb, wu_s[...])
        hb = (gate * jax.nn.sigmoid(gate) * up).astype(BF16)
        y = _pack_pairs(_dot(hb, wd_s[...]))
        when_chunk(g - N_YBUF, lambda g2: out_copy(g2).wait())
        ybuf[g % N_YBUF] = y
        out_copy(g).start()
        return carry
    lax.fori_loop(0, nch_ref[e], chunk, 0)

    @pl.when(e == pl.num_programs(0) - 1)
    def _():
        for back in range(N_YBUF, 0, -1):
            when_chunk(total - back, lambda g: out_copy(g).wait())


def _experts(gstart, nch, cstart, crows, total, xs, w_gate, w_up, w_down, layer):
    _, E, D, FF = w_gate.shape
    wspec = lambda a, b: pl.BlockSpec((None, None, a, b), lambda e, *_: (layer, e, 0, 0))
    grid_spec = pltpu.PrefetchScalarGridSpec(
        num_scalar_prefetch=5,
        grid=(E,),
        in_specs=[pl.BlockSpec(memory_space=pl.ANY), wspec(D, FF), wspec(D, FF), wspec(FF, D)],
        out_specs=pl.BlockSpec(memory_space=pl.ANY),
        scratch_shapes=[
            pltpu.VMEM((N_XBUF, CH_EXP, D // 2), U32),
            pltpu.VMEM((N_YBUF, CH_EXP, D // 2), U32),
            pltpu.VMEM((D, FF), BF16),
            pltpu.VMEM((D, FF), BF16),
            pltpu.VMEM((FF, D), BF16),
            pltpu.SemaphoreType.DMA((N_XBUF,)),
            pltpu.SemaphoreType.DMA((N_YBUF,)),
        ],
    )
    return pl.pallas_call(
        _experts_kernel,
        grid_spec=grid_spec,
        out_shape=jax.ShapeDtypeStruct(xs.shape, U32),
        compiler_params=_params(1),
        name="experts",
    )(gstart, nch, cstart, crows, total, xs, w_gate, w_up, w_down)


def _combine_kernel(n8_ref, loff_ref, gbase_ref, n8_nxt_ref, loff_nxt_ref, gbase_nxt_ref,
                    ys_ref, loc_ref, w_ref, h_ref, x1_ref, g2_ref,
                    wsg_ref, wsu_ref, wsd_ref, l2g_ref, l2b_ref, o_ref, yp_ref, sems, *, alpha):
    i = pl.program_id(0)
    rows, tm = yp_ref.shape[1], h_ref.shape[0]
    slot = i % 2

    def fetch(lo_ref, gb_ref, s):
        def copy(e, n):
            return _run_copy(ys_ref, pl.multiple_of(gb_ref[e], SUBLANES),
                             yp_ref.at[s], pl.multiple_of(lo_ref[e], SUBLANES), n, sems.at[s])
        return copy

    @pl.when(i == 0)
    def _():
        yp_ref[...] = jnp.zeros_like(yp_ref)
        cur = fetch(loff_ref, gbase_ref, slot)
        _for_each_run(n8_ref, lambda e, n: cur(e, n).start())

    @pl.when(i + 1 < pl.num_programs(0))
    def _():
        nxt = fetch(loff_nxt_ref, gbase_nxt_ref, 1 - slot)
        _for_each_run(n8_nxt_ref, lambda e, n: nxt(e, n).start())

    hb = h_ref[...]
    g = _dot(hb, wsg_ref[...])
    u = _dot(hb, wsu_ref[...])
    y = _dot((g * jax.nn.sigmoid(g) * u).astype(BF16), wsd_ref[...])

    loc = loc_ref[...]
    w = w_ref[...]
    pws = []
    for r0 in range(0, rows, PERM_BLOCK):
        lane = (lax.broadcasted_iota(I32, (tm, PERM_BLOCK), 1) + r0).astype(I16)
        pw = jnp.zeros((tm, PERM_BLOCK), BF16)
        for k in range(TOP_K):
            lk = jnp.broadcast_to(loc[:, k:k + 1], (tm, PERM_BLOCK)).astype(I16)
            wk = jnp.broadcast_to(w[:, k:k + 1], (tm, PERM_BLOCK)).astype(BF16)
            pw = jnp.where(lane == lk, wk, pw)
        pws.append(pw)

    _wait_all_runs(n8_ref, yp_ref.at[slot], ys_ref, sems.at[slot])

    for j, r0 in enumerate(range(0, rows, PERM_BLOCK)):
        yp = _unpack_pairs(yp_ref[slot, r0:r0 + PERM_BLOCK, :])
        y = y + _dot(pws[j], yp)
    o_ref[...] = _layer_norm(alpha * x1_ref[...] + g2_ref[0] * y, l2g_ref[...], l2b_ref[...])


def _combine(n8_flat, loff_flat, gbase_flat, ys, loc_tk, w_tk, h2b, x1, gate2, wsg, wsu, wsd,
             l2g, l2b, B, S, alpha):
    T, D = h2b.shape
    tm = min(TM_ROUTE, S)
    npb = S // tm
    rows = _tile_rows(tm)
    const = lambda i: (0, 0)
    full = lambda a: pl.BlockSpec(a.shape, const)
    n_tiles = T // tm
    per_tile = pl.BlockSpec((SMEM_ROW,), lambda i: (i,), memory_space=pltpu.SMEM)
    next_tile = pl.BlockSpec((SMEM_ROW,), lambda i: (jnp.minimum(i + 1, n_tiles - 1),),
                             memory_space=pltpu.SMEM)
    return pl.pallas_call(
        functools.partial(_combine_kernel, alpha=alpha),
        grid=(n_tiles,),
        in_specs=[
            per_tile, per_tile, per_tile, next_tile, next_tile, next_tile,
            pl.BlockSpec(memory_space=pl.ANY),
            pl.BlockSpec((tm, TOP_K), lambda i: (i, 0)),
            pl.BlockSpec((tm, TOP_K), lambda i: (i, 0)),
            pl.BlockSpec((tm, D), lambda i: (i, 0)),
            pl.BlockSpec((tm, D), lambda i: (i, 0)),
            pl.BlockSpec((1, 1, D), lambda i: (i // npb, 0, 0)),
            full(wsg), full(wsu), full(wsd), full(l2g), full(l2b),
        ],
        out_specs=pl.BlockSpec((tm, D), lambda i: (i, 0)),
        out_shape=jax.ShapeDtypeStruct((T, D), F32),
        scratch_shapes=[pltpu.VMEM((2, rows, D // 2), U32), pltpu.SemaphoreType.DMA((2,))],
        compiler_params=_params(1),
        name="combine",
    )(n8_flat, loff_flat, gbase_flat, n8_flat, loff_flat, gbase_flat,
      ys, loc_tk, w_tk, h2b, x1, gate2, wsg, wsu, wsd, l2g, l2b)


def _rope_tables(S):
    rows = S // GRID_W
    t = jnp.arange(S)
    row = (t // GRID_W - rows // 2).astype(F32)
    col = (t % GRID_W - GRID_W // 2).astype(F32)
    n_pairs = HEAD_DIM // 4
    inv_freq = ROPE_THETA ** (-jnp.arange(n_pairs, dtype=F32) / n_pairs)
    ang = jnp.concatenate([row[:, None] * inv_freq, col[:, None] * inv_freq], -1)
    cos, sin = jnp.cos(ang), jnp.sin(ang)
    cos_i = jnp.repeat(cos, 2, axis=-1)
    sin_i = jnp.stack([-sin, sin], axis=-1).reshape(S, HEAD_DIM)
    reps = LANES // HEAD_DIM
    return jnp.tile(cos_i, (1, reps)), jnp.tile(sin_i, (1, reps))


def kernel(x, c, w_ada, b_ada, w_in, q_scale, k_scale, sgu_ln_g, sgu_ln_b, w_spatial, b_spatial,
           w_branch_a, w_branch_b, w_out, ln1_g, ln1_b, router_w, router_bias, w_gate, w_up, w_down,
           ws_gate, ws_up, ws_down, ln2_g, ln2_b):
    B, S, D = x.shape
    T = B * S
    L = w_ada.shape[0]
    alpha = (2 * L) ** 0.25
    tm = min(TM_ROUTE, S)
    n_tiles = T // tm
    n_rows = T * TOP_K + n_tiles * N_EXPERTS * SUBLANES
    max_chunks = n_rows // CH_EXP + N_EXPERTS

    cos2, sin2 = _rope_tables(S)
    mod = _modulation(c, w_ada, b_ada)
    hd = jnp.arange(ATTN_WIDTH) // HEAD_DIM
    gsum = (hd[:, None] == hd[None, :]).astype(BF16)
    tri = (jnp.arange(tm)[:, None] < jnp.arange(tm)[None, :]).astype(BF16)
    etri = (jnp.arange(N_EXPERTS)[None, :] < jnp.arange(N_EXPERTS)[:, None]).astype(BF16)

    x2 = x.reshape(T, D)
    for l in range(L):
        m6 = mod[l].reshape(B, 1, 6, D)
        shift1, scale1, gate1, shift2, scale2, gate2 = [m6[:, :, j, :] for j in range(6)]
        qs = jnp.tile(q_scale[l], N_Q_HEADS).reshape(1, ATTN_WIDTH)
        ks = jnp.tile(k_scale[l], N_KV_HEADS).reshape(1, KV_WIDTH)
        q4, kt, vt, rest = _inproj(x2, scale1, shift1, w_in[l].astype(BF16), cos2, sin2, qs, ks,
                                   gsum, B, S)
        oa = _attention(q4, kt, vt, B, S)

        wcat = jnp.transpose(w_spatial[l], (1, 0, 2)).reshape(SGU_CHUNK, SGU_GROUPS * SGU_CHUNK)
        bexp = jnp.repeat(b_spatial[l].T, SGU_GROUP_DIM, axis=1)
        rw_t = router_w[l].T
        rwh = rw_t.astype(BF16)
        rwl = (rw_t - rwh.astype(F32)).astype(BF16)
        rb = jnp.broadcast_to(router_bias[l].astype(F32)[:, None], (N_EXPERTS, LANES))
        x1, h2b, loc_t, wts_t, n8_out = _mix(
            rest, oa, x2, gate1, scale2, shift2,
            sgu_ln_g[l].reshape(1, -1), sgu_ln_b[l].reshape(1, -1), wcat.astype(BF16), bexp,
            w_branch_a[l].astype(BF16), w_branch_b[l].astype(BF16), w_out[l].astype(BF16),
            ln1_g[l].reshape(1, -1), ln1_b[l].reshape(1, -1), rwh, rwl, rb, tri, etri, B, S, alpha)

        n8 = n8_out[:, :, 0]
        tot = jnp.sum(n8, axis=0)
        pstarts = jnp.cumsum(tot) - tot
        gbase = pstarts[None, :] + jnp.cumsum(n8, axis=0) - n8
        loff = jnp.cumsum(n8, axis=1) - n8
        n8_flat, loff_flat, gbase_flat = [_smem_table(a) for a in (n8, loff, gbase)]
        nch = (tot + CH_EXP - 1) // CH_EXP
        gend = jnp.cumsum(nch)
        gstart = gend - nch
        gidx = jnp.arange(max_chunks, dtype=I32)
        owner = (gend[None, :] <= gidx[:, None]).astype(I32).sum(axis=1)
        onehot = (owner[:, None] == jnp.arange(N_EXPERTS)[None, :]).astype(I32)
        pick = lambda v: jnp.sum(onehot * v[None, :], axis=1)
        within = (gidx - pick(gstart)) * CH_EXP
        crows = jnp.clip(pick(tot) - within, 0, CH_EXP).astype(I32)
        cstart = (pick(pstarts) + within).astype(I32)
        total = gend[-1].astype(I32).reshape(1)

        xs = _dispatch(n8_flat, loff_flat, gbase_flat, h2b, loc_t, n_rows)
        ys = _experts(gstart.astype(I32), nch.astype(I32), cstart, crows, total, xs,
                      w_gate, w_up, w_down, l)
        x2 = _combine(n8_flat, loff_flat, gbase_flat, ys, loc_t.T, wts_t.T, h2b, x1, gate2,
                      ws_gate[l].astype(BF16), ws_up[l].astype(BF16), ws_down[l].astype(BF16),
                      ln2_g[l].reshape(1, -1), ln2_b[l].reshape(1, -1), B, S, alpha)
    return x2.reshape(B, S, D)
```

```python
import functools

import jax
import jax.numpy as jnp
from jax import lax
from jax.experimental import pallas as pl
from jax.experimental.pallas import tpu as pltpu

F32 = jnp.float32
BF16 = jnp.bfloat16
I32 = jnp.int32
I16 = jnp.int16
U32 = jnp.uint32

GRID_W = 64
N_Q_HEADS = 8
N_KV_HEADS = 2
HEAD_DIM = 64
ROPE_THETA = 10000.0
ATTN_WIDTH = N_Q_HEADS * HEAD_DIM
KV_WIDTH = N_KV_HEADS * HEAD_DIM
SGU_GROUPS = 8
SGU_WIDTH = 512
SGU_GROUP_DIM = SGU_WIDTH // SGU_GROUPS
SGU_CHUNK = 128
N_EXPERTS = 64
TOP_K = 8
N_GROUPS = 8
TOPK_GROUPS = 4
GROUP_SIZE = N_EXPERTS // N_GROUPS
ROUTED_SCALE = 2.5
LN_EPS = 1e-5
RMS_EPS = 1e-6

LANES = 128
SUBLANES = 8
VMEM_LIMIT = 56 * 1024 * 1024

TM_IN = 512
TQ = 256
KV_CHUNK = 512
LOG2_E = 1.4426950408889634
TM_ROUTE = 256
TM_MIX = 512
CH_EXP = 512
CH_CLASS = 128
PERM_BLOCK = 256


def _params(n_axes):
    return pltpu.CompilerParams(
        dimension_semantics=("arbitrary",) * n_axes, vmem_limit_bytes=VMEM_LIMIT)


def _dot(a, b):
    return jnp.dot(a, b, preferred_element_type=F32)


def _dot_nt(a, b):
    return lax.dot_general(a, b, (((1,), (1,)), ((), ())), preferred_element_type=F32)


def _split_bf16(a):
    hi = a.astype(BF16)
    lo = (a - hi.astype(F32)).astype(BF16)
    return hi, lo


def _layer_norm(r, g, b):
    mu = jnp.mean(r, axis=-1, keepdims=True)
    d = r - mu
    var = jnp.mean(d * d, axis=-1, keepdims=True)
    return d * lax.rsqrt(var + LN_EPS) * g + b


def _pack_pairs(y):
    n = y.shape[1] // 2
    hi = lax.bitcast_convert_type(y[:, :n].astype(BF16).astype(F32), U32)
    lo = lax.bitcast_convert_type(y[:, n:].astype(BF16).astype(F32), U32)
    return hi | (lo >> 16)


def _unpack_pairs(w):
    hi = lax.bitcast_convert_type(w & jnp.uint32(0xFFFF0000), F32).astype(BF16)
    lo = lax.bitcast_convert_type(w << 16, F32).astype(BF16)
    return jnp.concatenate([hi, lo], axis=1)


def _mod_kernel(c_ref, w_ref, b_ref, o_ref):
    c = c_ref[...]
    cond = c * jax.nn.sigmoid(c)
    o_ref[...] = _dot(cond, w_ref[...]) + b_ref[...]


def _modulation(c, w_ada, b_ada):
    L, D, D6 = w_ada.shape
    B = c.shape[0]
    nj = D6 // D
    return pl.pallas_call(
        _mod_kernel,
        grid=(L, nj),
        in_specs=[
            pl.BlockSpec((B, D), lambda l, j: (0, 0)),
            pl.BlockSpec((None, D, D), lambda l, j: (l, 0, j)),
            pl.BlockSpec((None, 1, D), lambda l, j: (l, 0, j)),
        ],
        out_specs=pl.BlockSpec((None, B, D), lambda l, j: (l, 0, j)),
        out_shape=jax.ShapeDtypeStruct((L, B, D6), F32),
        compiler_params=_params(2),
        name="modulation",
    )(c, w_ada, b_ada.reshape(L, 1, D6))


def _swap_pairs(a):
    n = a.shape[1]
    lane = lax.broadcasted_iota(I32, a.shape, 1)
    nxt = pltpu.roll(a, n - 1, axis=1)
    prv = pltpu.roll(a, 1, axis=1)
    return jnp.where((lane & 1) == 0, nxt, prv)


def _head_rms(p, gsum_ref, width):
    sq = p * p
    hi, lo = _split_bf16(sq)
    g = gsum_ref[0:width, 0:width]
    ss = _dot(hi, g) + _dot(lo, g)
    return lax.rsqrt(ss * (1.0 / HEAD_DIM) + RMS_EPS)


def _inproj_kernel(x_ref, sc_ref, sh_ref, w_ref, cos_ref, sin_ref, qs_ref, ks_ref, gsum_ref,
                   q_ref, kt_ref, vt_ref, r_ref):
    x = x_ref[...]
    h = (x * (1.0 + sc_ref[0]) + sh_ref[0]).astype(BF16)
    cos2 = cos_ref[...]
    sin2 = sin_ref[...]
    pq = _dot(h, w_ref[:, 0:ATTN_WIDTH])
    qn = pq * _head_rms(pq, gsum_ref, ATTN_WIDTH) * qs_ref[...]
    cos8 = jnp.concatenate([cos2] * (ATTN_WIDTH // LANES), axis=1)
    sin8 = jnp.concatenate([sin2] * (ATTN_WIDTH // LANES), axis=1)
    qr = (qn * cos8 + _swap_pairs(qn) * sin8) * (HEAD_DIM ** -0.5 * LOG2_E)
    for p in range(ATTN_WIDTH // LANES):
        q_ref[p] = qr[:, p * LANES:(p + 1) * LANES].astype(BF16)
    k0 = ATTN_WIDTH
    pk = _dot(h, w_ref[:, k0:k0 + KV_WIDTH])
    kn = pk * _head_rms(pk, gsum_ref, KV_WIDTH) * ks_ref[...]
    kr = kn * cos2 + _swap_pairs(kn) * sin2
    kt_ref[...] = kr.T.astype(BF16)
    v0 = k0 + KV_WIDTH
    pv = _dot(h, w_ref[:, v0:v0 + KV_WIDTH])
    vt_ref[...] = pv.T.astype(BF16)
    r0 = v0 + KV_WIDTH
    n_rest = r_ref.shape[1]
    step = 1024
    for j in range(n_rest // step):
        r_ref[:, j * step:(j + 1) * step] = _dot(
            h, w_ref[:, r0 + j * step:r0 + (j + 1) * step]).astype(BF16)


def _inproj(x2, scale1, shift1, w_in_b, cos2, sin2, qs, ks, gsum, B, S):
    T, D = x2.shape
    tm = min(TM_IN, S)
    npb = S // tm
    n_rest = w_in_b.shape[1] - ATTN_WIDTH - 2 * KV_WIDTH
    nq = ATTN_WIDTH // LANES
    const = lambda i: (0, 0)
    return pl.pallas_call(
        _inproj_kernel,
        grid=(T // tm,),
        in_specs=[
            pl.BlockSpec((tm, D), lambda i: (i, 0)),
            pl.BlockSpec((1, 1, D), lambda i: (i // npb, 0, 0)),
            pl.BlockSpec((1, 1, D), lambda i: (i // npb, 0, 0)),
            pl.BlockSpec(w_in_b.shape, const),
            pl.BlockSpec((tm, LANES), lambda i: (i % npb, 0)),
            pl.BlockSpec((tm, LANES), lambda i: (i % npb, 0)),
            pl.BlockSpec((1, ATTN_WIDTH), const),
            pl.BlockSpec((1, KV_WIDTH), const),
            pl.BlockSpec(gsum.shape, const),
        ],
        out_specs=[
            pl.BlockSpec((nq, tm, LANES), lambda i: (0, i, 0)),
            pl.BlockSpec((None, KV_WIDTH, tm), lambda i: (i // npb, 0, i % npb)),
            pl.BlockSpec((None, KV_WIDTH, tm), lambda i: (i // npb, 0, i % npb)),
            pl.BlockSpec((tm, n_rest), lambda i: (i, 0)),
        ],
        out_shape=[
            jax.ShapeDtypeStruct((nq, T, LANES), BF16),
            jax.ShapeDtypeStruct((B, KV_WIDTH, S), BF16),
            jax.ShapeDtypeStruct((B, KV_WIDTH, S), BF16),
            jax.ShapeDtypeStruct((T, n_rest), BF16),
        ],
        compiler_params=_params(1),
        name="inproj",
    )(x2, scale1, shift1, w_in_b, cos2, sin2, qs, ks, gsum)


def _attn_kernel(q_ref, kt_ref, vt_ref, o_ref, *s_refs):
    kt = kt_ref[...]
    vt = vt_ref[...]
    tq, S = s_refs[0].shape
    kc = min(KV_CHUNK, S)
    heads = []
    for pp in range(q_ref.shape[0]):
        q2 = q_ref[pp]
        for hh in range(LANES // HEAD_DIM):
            heads.append(q2[:, hh * HEAD_DIM:(hh + 1) * HEAD_DIM])
    s_refs[0][...] = _dot(heads[0], kt)
    outs = []
    for h, s_ref in enumerate(s_refs):
        if h + 1 < len(heads):
            s_refs[h + 1][...] = _dot(heads[h + 1], kt)
        m = jnp.max(s_ref[...], axis=-1, keepdims=True)
        l = jnp.zeros_like(m)
        acc = jnp.zeros((tq, HEAD_DIM), F32)
        for c in range(S // kc):
            p = jnp.exp2(s_ref[:, c * kc:(c + 1) * kc] - m)
            l = l + jnp.sum(p, axis=-1, keepdims=True)
            acc = acc + _dot_nt(p.astype(BF16), vt[:, c * kc:(c + 1) * kc])
        outs.append(acc / l)
    o_ref[...] = jnp.concatenate(outs, axis=1).astype(BF16)


def _attention(q4, kt, vt, B, S):
    nq, T, _ = q4.shape
    tq = min(TQ, S)
    nqb = S // tq
    pairs_per_kv = nq // N_KV_HEADS
    heads_per_kv = N_Q_HEADS // N_KV_HEADS
    return pl.pallas_call(
        _attn_kernel,
        grid=(B, N_KV_HEADS, nqb),
        in_specs=[
            pl.BlockSpec((pairs_per_kv, tq, LANES), lambda b, g, i: (g, b * nqb + i, 0)),
            pl.BlockSpec((None, HEAD_DIM, S), lambda b, g, i: (b, g, 0)),
            pl.BlockSpec((None, HEAD_DIM, S), lambda b, g, i: (b, g, 0)),
        ],
        out_specs=pl.BlockSpec((tq, pairs_per_kv * LANES), lambda b, g, i: (b * nqb + i, g)),
        out_shape=jax.ShapeDtypeStruct((T, ATTN_WIDTH), BF16),
        scratch_shapes=[pltpu.VMEM((tq, S), F32)] * heads_per_kv,
        compiler_params=_params(3),
        name="attention",
    )(q4, kt, vt)


def _gelu(a):
    return 0.5 * a * (1.0 + lax.erf(a * (2.0 ** -0.5)))


def _route(sel_t, scores_t):
    tm = sel_t.shape[1]
    neg = -jnp.inf
    row8 = lax.broadcasted_iota(I32, (GROUP_SIZE, tm), 0)
    groups = [sel_t[g * GROUP_SIZE:(g + 1) * GROUP_SIZE, :] for g in range(N_GROUPS)]
    gscore = []
    for a in groups:
        m1 = jnp.max(a, axis=0, keepdims=True)
        i1 = jnp.min(jnp.where(a == m1, row8, GROUP_SIZE), axis=0, keepdims=True)
        m2 = jnp.max(jnp.where(row8 == i1, neg, a), axis=0, keepdims=True)
        gscore.append(m1 + m2)
    masked = []
    for g in range(N_GROUPS):
        rank = jnp.zeros((1, tm), I32)
        for g2 in range(N_GROUPS):
            if g2 == g:
                continue
            if g2 < g:
                ahead = gscore[g2] >= gscore[g]
            else:
                ahead = gscore[g2] > gscore[g]
            rank = rank + ahead.astype(I32)
        keep = rank < TOPK_GROUPS
        masked.append(jnp.where(keep, groups[g], neg))
    cur = jnp.concatenate(masked, axis=0)
    row = lax.broadcasted_iota(I32, (N_EXPERTS, tm), 0)
    wts, hits = [], []
    chosen = jnp.zeros((N_EXPERTS, tm), F32)
    for _ in range(TOP_K):
        m = jnp.max(cur, axis=0, keepdims=True)
        idx = jnp.min(jnp.where(cur == m, row, N_EXPERTS), axis=0, keepdims=True)
        hit = row == idx
        wts.append(jnp.sum(jnp.where(hit, scores_t, 0.0), axis=0, keepdims=True))
        hits.append(hit)
        chosen = jnp.where(hit, 1.0, chosen)
        cur = jnp.where(hit, neg, cur)
    return wts, hits, chosen


def _mix_kernel(z_ref, ga_ref, gb_ref, oa_ref, x_ref, g1_ref, sc2_ref, sh2_ref,
                sg_ref, sb_ref, wcat_ref, bexp_ref, wa_ref, wb_ref, wo_ref,
                l1g_ref, l1b_ref, rwh_ref, rwl_ref, rb_ref, tri_ref, etri_ref,
                x1_ref, h2_ref, loc_ref, wts_ref, n8_ref, *, alpha):
    row_slices = [slice(r0, r0 + TM_ROUTE) for r0 in range(0, z_ref.shape[0], TM_ROUTE)]
    h2s = [_mix_dense(rs, z_ref, ga_ref, gb_ref, oa_ref, x_ref, g1_ref, sc2_ref, sh2_ref,
                      sg_ref, sb_ref, wcat_ref, bexp_ref, wa_ref, wb_ref, wo_ref,
                      l1g_ref, l1b_ref, x1_ref, h2_ref, alpha) for rs in row_slices]
    for sub, rs in enumerate(row_slices):
        _mix_route(sub, rs, h2s[sub], rwh_ref, rwl_ref, rb_ref, tri_ref, etri_ref,
                   loc_ref, wts_ref, n8_ref)


def _mix_dense(rs, z_ref, ga_ref, gb_ref, oa_ref, x_ref, g1_ref, sc2_ref, sh2_ref,
               sg_ref, sb_ref, wcat_ref, bexp_ref, wa_ref, wb_ref, wo_ref,
               l1g_ref, l1b_ref, x1_ref, h2_ref, alpha):
    tm = TM_ROUTE
    z = _gelu(z_ref[rs, :].astype(F32))
    u = z[:, :SGU_WIDTH]
    v = _layer_norm(z[:, SGU_WIDTH:], sg_ref[...], sb_ref[...])
    lane = lax.broadcasted_iota(I32, (SGU_CHUNK, SGU_WIDTH), 1)
    wcat = wcat_ref[...]
    svs = []
    for c in range(tm // SGU_CHUNK):
        vc = v[c * SGU_CHUNK:(c + 1) * SGU_CHUNK, :]
        stacked = jnp.concatenate(
            [jnp.where(lane // SGU_GROUP_DIM == g, vc, 0.0) for g in range(SGU_GROUPS)],
            axis=0).astype(BF16)
        svs.append(_dot(wcat, stacked) + bexp_ref[...])
    sv = jnp.concatenate(svs, axis=0) if len(svs) > 1 else svs[0]
    ob = (u * sv).astype(BF16)
    ya = _dot(oa_ref[rs, :], wa_ref[...])
    yb = _dot(ob, wb_ref[...])
    y = (jax.nn.sigmoid(ga_ref[rs, :].astype(F32)) * ya
         + jax.nn.sigmoid(gb_ref[rs, :].astype(F32)) * yb)
    out = _dot(y.astype(BF16), wo_ref[...])
    x1 = _layer_norm(alpha * x_ref[rs, :] + g1_ref[0] * out, l1g_ref[...], l1b_ref[...])
    x1_ref[rs, :] = x1
    h2 = x1 * (1.0 + sc2_ref[0]) + sh2_ref[0]
    h2_ref[rs, :] = h2.astype(BF16)
    return h2


def _mix_route(sub, rs, h2, rwh_ref, rwl_ref, rb_ref, tri_ref, etri_ref, loc_ref, wts_ref, n8_ref):
    hh, hl = _split_bf16(h2)
    rwh = rwh_ref[...]
    logits_t = _dot_nt(rwh, hh) + _dot_nt(rwh, hl) + _dot_nt(rwl_ref[...], hh)
    scores_t = jax.nn.sigmoid(logits_t)
    sel_t = scores_t + rb_ref[:, 0:1]
    wts, hits, chosen = _route(sel_t, scores_t)
    wsum = wts[0]
    for w in wts[1:]:
        wsum = wsum + w
    inv = ROUTED_SCALE / wsum
    rank = _dot(chosen.astype(BF16), tri_ref[...])
    n = jnp.sum(chosen, axis=1, keepdims=True)
    n8 = jnp.floor((n + (SUBLANES - 1)) * (1.0 / SUBLANES)) * SUBLANES
    n8_wide = jnp.broadcast_to(n8, (N_EXPERTS, LANES))
    run_off = _dot(etri_ref[...], n8_wide.astype(BF16))
    slot = rank + run_off[:, 0:1]
    locs = [jnp.sum(jnp.where(h, slot, 0.0), axis=0, keepdims=True) for h in hits]
    loc_ref[:, rs] = jnp.concatenate(locs, axis=0).astype(I32)
    wts_ref[:, rs] = jnp.concatenate([w * inv for w in wts], axis=0)
    n8_ref[sub] = n8_wide.astype(I32)


def _mix(rest, oa, x2, gate1, scale2, shift2, sg, sb, wcat, bexp, wa, wb, wo, l1g, l1b,
         rwh, rwl, rb, tri, etri, B, S, alpha):
    T, D = x2.shape
    tm = min(TM_MIX, S)
    nsub = tm // TM_ROUTE
    npb = S // tm
    const = lambda i: (0, 0)
    modspec = pl.BlockSpec((1, 1, D), lambda i: (i // npb, 0, 0))
    full = lambda a: pl.BlockSpec(a.shape, const)
    return pl.pallas_call(
        functools.partial(_mix_kernel, alpha=alpha),
        grid=(T // tm,),
        in_specs=[
            pl.BlockSpec((tm, D), lambda i: (i, 0)),
            pl.BlockSpec((tm, D), lambda i: (i, 1)),
            pl.BlockSpec((tm, D), lambda i: (i, 2)),
            pl.BlockSpec((tm, ATTN_WIDTH), lambda i: (i, 0)),
            pl.BlockSpec((tm, D), lambda i: (i, 0)),
            modspec, modspec, modspec,
            full(sg), full(sb), full(wcat), full(bexp), full(wa), full(wb), full(wo),
            full(l1g), full(l1b), full(rwh), full(rwl), full(rb), full(tri), full(etri),
        ],
        out_specs=[
            pl.BlockSpec((tm, D), lambda i: (i, 0)),
            pl.BlockSpec((tm, D), lambda i: (i, 0)),
            pl.BlockSpec((TOP_K, tm), lambda i: (0, i)),
            pl.BlockSpec((TOP_K, tm), lambda i: (0, i)),
            pl.BlockSpec((nsub, N_EXPERTS, LANES), lambda i: (i, 0, 0)),
        ],
        out_shape=[
            jax.ShapeDtypeStruct((T, D), F32),
            jax.ShapeDtypeStruct((T, D), BF16),
            jax.ShapeDtypeStruct((TOP_K, T), I32),
            jax.ShapeDtypeStruct((TOP_K, T), F32),
            jax.ShapeDtypeStruct((T // TM_ROUTE, N_EXPERTS, LANES), I32),
        ],
        compiler_params=_params(1),
        name="mix",
    )(rest, rest, rest, oa, x2, gate1, scale2, shift2, sg, sb, wcat, bexp, wa, wb, wo,
      l1g, l1b, rwh, rwl, rb, tri, etri)


SMEM_ROW = 128


def _smem_table(a):
    a = a.astype(I32)
    a = jnp.concatenate([a, jnp.sum(a, axis=1, keepdims=True)], axis=1)
    return jnp.pad(a, ((0, 0), (0, SMEM_ROW - a.shape[1]))).reshape(-1)


def _wait_all_runs(n_ref, vmem_ref, hbm_ref, sem):
    total = pl.multiple_of(n_ref[N_EXPERTS], SUBLANES)
    _run_copy(vmem_ref, 0, hbm_ref, 0, total, sem).wait()


def _tile_rows(tm):
    return tm * TOP_K + N_EXPERTS * SUBLANES


def _run_copy(src_ref, s, dst_ref, d, n, sem):
    return pltpu.make_async_copy(src_ref.at[pl.ds(s, n)], dst_ref.at[pl.ds(d, n)], sem)


def _for_each_run(n_ref, fn):
    for e in range(N_EXPERTS):
        n = pl.multiple_of(n_ref[e], SUBLANES)

        @pl.when(n > 0)
        def _():
            fn(e, n)


def _dispatch_kernel(n8_ref, loff_ref, gbase_ref, n8_prev_ref, h_ref, loc_ref,
                     xs_ref, xp_ref, sems):
    i = pl.program_id(0)
    rows, tm = xp_ref.shape[1], h_ref.shape[0]
    slot = i % 2

    loc16 = loc_ref[...].astype(I16)
    hb = h_ref[...]
    half = hb.shape[1] // 2
    for r0 in range(0, rows, PERM_BLOCK):
        row = (lax.broadcasted_iota(I32, (PERM_BLOCK, tm), 0) + r0).astype(I16)
        perm = jnp.zeros((PERM_BLOCK, tm), BF16)
        one = jnp.ones((PERM_BLOCK, tm), BF16)
        for k in range(TOP_K):
            perm = jnp.where(row == loc16[k:k + 1, :], one, perm)
        xp = _dot(perm, hb)
        xp_ref[slot, r0:r0 + PERM_BLOCK, :] = (
            lax.bitcast_convert_type(xp[:, :half], U32)
            | (lax.bitcast_convert_type(xp[:, half:], U32) >> 16))

    def run_copy(e, n):
        return _run_copy(xp_ref.at[slot], pl.multiple_of(loff_ref[e], SUBLANES),
                         xs_ref, pl.multiple_of(gbase_ref[e], SUBLANES), n, sems.at[slot])

    _for_each_run(n8_ref, lambda e, n: run_copy(e, n).start())

    @pl.when(i > 0)
    def _():
        _wait_all_runs(n8_prev_ref, xp_ref.at[1 - slot], xs_ref, sems.at[1 - slot])

    @pl.when(i == pl.num_programs(0) - 1)
    def _():
        _wait_all_runs(n8_ref, xp_ref.at[slot], xs_ref, sems.at[slot])


def _dispatch(n8_flat, loff_flat, gbase_flat, h2b, loc_t, n_rows):
    T, D = h2b.shape
    tm = min(TM_ROUTE, T)
    rows = _tile_rows(tm)
    per_tile = pl.BlockSpec((SMEM_ROW,), lambda i: (i,), memory_space=pltpu.SMEM)
    prev_tile = pl.BlockSpec((SMEM_ROW,), lambda i: (jnp.maximum(i - 1, 0),),
                             memory_space=pltpu.SMEM)
    return pl.pallas_call(
        _dispatch_kernel,
        grid=(T // tm,),
        in_specs=[
            per_tile, per_tile, per_tile, prev_tile,
            pl.BlockSpec((tm, D), lambda i: (i, 0)),
            pl.BlockSpec((TOP_K, tm), lambda i: (0, i)),
        ],
        out_specs=pl.BlockSpec(memory_space=pl.ANY),
        out_shape=jax.ShapeDtypeStruct((n_rows, D // 2), U32),
        scratch_shapes=[pltpu.VMEM((2, rows, D // 2), U32), pltpu.SemaphoreType.DMA((2,))],
        compiler_params=_params(1),
        name="dispatch",
    )(n8_flat, loff_flat, gbase_flat, n8_flat, h2b, loc_t)


N_XBUF = 3
N_YBUF = 2


def _experts_kernel(gstart_ref, nch_ref, cstart_ref, crows_ref, total_ref,
                    xs_ref, wg_ref, wu_ref, wd_ref, ys_ref,
                    xbuf, ybuf, wg_s, wu_s, wd_s, sem_in, sem_out):
    e = pl.program_id(0)
    total = total_ref[0]

    def in_copy(g):
        s = g % N_XBUF
        n = pl.multiple_of(crows_ref[g], SUBLANES)
        return _run_copy(xs_ref, pl.multiple_of(cstart_ref[g], SUBLANES), xbuf.at[s], 0, n,
                         sem_in.at[s])

    def out_copy(g):
        s = g % N_YBUF
        n = pl.multiple_of(crows_ref[g], SUBLANES)
        return _run_copy(ybuf.at[s], 0, ys_ref, pl.multiple_of(cstart_ref[g], SUBLANES), n,
                         sem_out.at[s])

    def when_chunk(g, fn):
        @pl.when(jnp.logical_and(g >= 0, g < total))
        def _():
            fn(g)

    @pl.when(e == 0)
    def _():
        xbuf[...] = jnp.zeros_like(xbuf)
        for g0 in range(N_XBUF - 1):
            when_chunk(g0, lambda g: in_copy(g).start())

    wg_s[...] = wg_ref[...].astype(BF16)
    wu_s[...] = wu_ref[...].astype(BF16)
    wd_s[...] = wd_ref[...].astype(BF16)

    def chunk(c, carry):
        g = gstart_ref[e] + c
        in_copy(g).wait()
        when_chunk(g + (N_XBUF - 1), lambda g2: in_copy(g2).start())
        when_chunk(g - N_YBUF, lambda g2: out_copy(g2).wait())
        xs_slot = xbuf.at[g % N_XBUF]
        ys_slot = ybuf.at[g % N_YBUF]
        rows = crows_ref[g]
        for m in range(CH_CLASS, CH_EXP + 1, CH_CLASS):
            @pl.when(jnp.logical_and(rows > m - CH_CLASS, rows <= m))
            def _():
                xb = _unpack_pairs(xs_slot[0:m, :])
                gate = _dot(xb, wg_s[...])
                up = _dot(xb, wu_s[...])
                hb = (gate * jax.nn.sigmoid(gate) * up).astype(BF16)
                ys_slot[0:m, :] = _pack_pairs(_dot(hb, wd_s[...]))
        out_copy(g).start()
        return carry
    lax.fori_loop(0, nch_ref[e], chunk, 0)

    @pl.when(e == pl.num_programs(0) - 1)
    def _():
        for back in range(N_YBUF, 0, -1):
            when_chunk(total - back, lambda g: out_copy(g).wait())


def _experts(gstart, nch, cstart, crows, total, xs, w_gate, w_up, w_down, layer):
    _, E, D, FF = w_gate.shape
    wspec = lambda a, b: pl.BlockSpec((None, None, a, b), lambda e, *_: (layer, e, 0, 0))
    grid_spec = pltpu.PrefetchScalarGridSpec(
        num_scalar_prefetch=5,
        grid=(E,),
        in_specs=[pl.BlockSpec(memory_space=pl.ANY), wspec(D, FF), wspec(D, FF), wspec(FF, D)],
        out_specs=pl.BlockSpec(memory_space=pl.ANY),
        scratch_shapes=[
            pltpu.VMEM((N_XBUF, CH_EXP, D // 2), U32),
            pltpu.VMEM((N_YBUF, CH_EXP, D // 2), U32),
            pltpu.VMEM((D, FF), BF16),
            pltpu.VMEM((D, FF), BF16),
            pltpu.VMEM((FF, D), BF16),
            pltpu.SemaphoreType.DMA((N_XBUF,)),
            pltpu.SemaphoreType.DMA((N_YBUF,)),
        ],
    )
    return pl.pallas_call(
        _experts_kernel,
        grid_spec=grid_spec,
        out_shape=jax.ShapeDtypeStruct(xs.shape, U32),
        compiler_params=_params(1),
        name="experts",
    )(gstart, nch, cstart, crows, total, xs, w_gate, w_up, w_down)


def _combine_kernel(n8_ref, loff_ref, gbase_ref, n8_nxt_ref, loff_nxt_ref, gbase_nxt_ref,
                    ys_ref, loc_ref, w_ref, h_ref, x1_ref, g2_ref,
                    wsg_ref, wsu_ref, wsd_ref, l2g_ref, l2b_ref, o_ref, yp_ref, sems, *, alpha):
    i = pl.program_id(0)
    rows, tm = yp_ref.shape[1], h_ref.shape[0]
    slot = i % 2

    def fetch(lo_ref, gb_ref, s):
        def copy(e, n):
            return _run_copy(ys_ref, pl.multiple_of(gb_ref[e], SUBLANES),
                             yp_ref.at[s], pl.multiple_of(lo_ref[e], SUBLANES), n, sems.at[s])
        return copy

    @pl.when(i == 0)
    def _():
        yp_ref[...] = jnp.zeros_like(yp_ref)
        cur = fetch(loff_ref, gbase_ref, slot)
        _for_each_run(n8_ref, lambda e, n: cur(e, n).start())

    @pl.when(i + 1 < pl.num_programs(0))
    def _():
        nxt = fetch(loff_nxt_ref, gbase_nxt_ref, 1 - slot)
        _for_each_run(n8_nxt_ref, lambda e, n: nxt(e, n).start())

    hb = h_ref[...]
    g = _dot(hb, wsg_ref[...])
    u = _dot(hb, wsu_ref[...])
    y = _dot((g * jax.nn.sigmoid(g) * u).astype(BF16), wsd_ref[...])

    loc = loc_ref[...]
    w = w_ref[...]
    pws = []
    for r0 in range(0, rows, PERM_BLOCK):
        lane = (lax.broadcasted_iota(I32, (tm, PERM_BLOCK), 1) + r0).astype(I16)
        pw = jnp.zeros((tm, PERM_BLOCK), BF16)
        for k in range(TOP_K):
            lk = jnp.broadcast_to(loc[:, k:k + 1], (tm, PERM_BLOCK)).astype(I16)
            wk = jnp.broadcast_to(w[:, k:k + 1], (tm, PERM_BLOCK)).astype(BF16)
            pw = jnp.where(lane == lk, wk, pw)
        pws.append(pw)

    _wait_all_runs(n8_ref, yp_ref.at[slot], ys_ref, sems.at[slot])

    for j, r0 in enumerate(range(0, rows, PERM_BLOCK)):
        yp = _unpack_pairs(yp_ref[slot, r0:r0 + PERM_BLOCK, :])
        y = y + _dot(pws[j], yp)
    o_ref[...] = _layer_norm(alpha * x1_ref[...] + g2_ref[0] * y, l2g_ref[...], l2b_ref[...])


def _combine(n8_flat, loff_flat, gbase_flat, ys, loc_tk, w_tk, h2b, x1, gate2, wsg, wsu, wsd,
             l2g, l2b, B, S, alpha):
    T, D = h2b.shape
    tm = min(TM_ROUTE, S)
    npb = S // tm
    rows = _tile_rows(tm)
    const = lambda i: (0, 0)
    full = lambda a: pl.BlockSpec(a.shape, const)
    n_tiles = T // tm
    per_tile = pl.BlockSpec((SMEM_ROW,), lambda i: (i,), memory_space=pltpu.SMEM)
    next_tile = pl.BlockSpec((SMEM_ROW,), lambda i: (jnp.minimum(i + 1, n_tiles - 1),),
                             memory_space=pltpu.SMEM)
    return pl.pallas_call(
        functools.partial(_combine_kernel, alpha=alpha),
        grid=(n_tiles,),
        in_specs=[
            per_tile, per_tile, per_tile, next_tile, next_tile, next_tile,
            pl.BlockSpec(memory_space=pl.ANY),
            pl.BlockSpec((tm, TOP_K), lambda i: (i, 0)),
            pl.BlockSpec((tm, TOP_K), lambda i: (i, 0)),
            pl.BlockSpec((tm, D), lambda i: (i, 0)),
            pl.BlockSpec((tm, D), lambda i: (i, 0)),
            pl.BlockSpec((1, 1, D), lambda i: (i // npb, 0, 0)),
            full(wsg), full(wsu), full(wsd), full(l2g), full(l2b),
        ],
        out_specs=pl.BlockSpec((tm, D), lambda i: (i, 0)),
        out_shape=jax.ShapeDtypeStruct((T, D), F32),
        scratch_shapes=[pltpu.VMEM((2, rows, D // 2), U32), pltpu.SemaphoreType.DMA((2,))],
        compiler_params=_params(1),
        name="combine",
    )(n8_flat, loff_flat, gbase_flat, n8_flat, loff_flat, gbase_flat,
      ys, loc_tk, w_tk, h2b, x1, gate2, wsg, wsu, wsd, l2g, l2b)


def _rope_tables(S):
    rows = S // GRID_W
    t = jnp.arange(S)
    row = (t // GRID_W - rows // 2).astype(F32)
    col = (t % GRID_W - GRID_W // 2).astype(F32)
    n_pairs = HEAD_DIM // 4
    inv_freq = ROPE_THETA ** (-jnp.arange(n_pairs, dtype=F32) / n_pairs)
    ang = jnp.concatenate([row[:, None] * inv_freq, col[:, None] * inv_freq], -1)
    cos, sin = jnp.cos(ang), jnp.sin(ang)
    cos_i = jnp.repeat(cos, 2, axis=-1)
    sin_i = jnp.stack([-sin, sin], axis=-1).reshape(S, HEAD_DIM)
    reps = LANES // HEAD_DIM
    return jnp.tile(cos_i, (1, reps)), jnp.tile(sin_i, (1, reps))


def kernel(x, c, w_ada, b_ada, w_in, q_scale, k_scale, sgu_ln_g, sgu_ln_b, w_spatial, b_spatial,
           w_branch_a, w_branch_b, w_out, ln1_g, ln1_b, router_w, router_bias, w_gate, w_up, w_down,
           ws_gate, ws_up, ws_down, ln2_g, ln2_b):
    B, S, D = x.shape
    T = B * S
    L = w_ada.shape[0]
    alpha = (2 * L) ** 0.25
    tm = min(TM_ROUTE, S)
    n_tiles = T // tm
    n_rows = T * TOP_K + n_tiles * N_EXPERTS * SUBLANES
    max_chunks = n_rows // CH_EXP + N_EXPERTS

    cos2, sin2 = _rope_tables(S)
    mod = _modulation(c, w_ada, b_ada)
    hd = jnp.arange(ATTN_WIDTH) // HEAD_DIM
    gsum = (hd[:, None] == hd[None, :]).astype(BF16)
    tri = (jnp.arange(tm)[:, None] < jnp.arange(tm)[None, :]).astype(BF16)
    etri = (jnp.arange(N_EXPERTS)[None, :] < jnp.arange(N_EXPERTS)[:, None]).astype(BF16)

    x2 = x.reshape(T, D)
    for l in range(L):
        m6 = mod[l].reshape(B, 1, 6, D)
        shift1, scale1, gate1, shift2, scale2, gate2 = [m6[:, :, j, :] for j in range(6)]
        qs = jnp.tile(q_scale[l], N_Q_HEADS).reshape(1, ATTN_WIDTH)
        ks = jnp.tile(k_scale[l], N_KV_HEADS).reshape(1, KV_WIDTH)
        q4, kt, vt, rest = _inproj(x2, scale1, shift1, w_in[l].astype(BF16), cos2, sin2, qs, ks,
                                   gsum, B, S)
        oa = _attention(q4, kt, vt, B, S)

        wcat = jnp.transpose(w_spatial[l], (1, 0, 2)).reshape(SGU_CHUNK, SGU_GROUPS * SGU_CHUNK)
        bexp = jnp.repeat(b_spatial[l].T, SGU_GROUP_DIM, axis=1)
        rw_t = router_w[l].T
        rwh = rw_t.astype(BF16)
        rwl = (rw_t - rwh.astype(F32)).astype(BF16)
        rb = jnp.broadcast_to(router_bias[l].astype(F32)[:, None], (N_EXPERTS, LANES))
        x1, h2b, loc_t, wts_t, n8_out = _mix(
            rest, oa, x2, gate1, scale2, shift2,
            sgu_ln_g[l].reshape(1, -1), sgu_ln_b[l].reshape(1, -1), wcat.astype(BF16), bexp,
            w_branch_a[l].astype(BF16), w_branch_b[l].astype(BF16), w_out[l].astype(BF16),
            ln1_g[l].reshape(1, -1), ln1_b[l].reshape(1, -1), rwh, rwl, rb, tri, etri, B, S, alpha)

        n8 = n8_out[:, :, 0]
        tot = jnp.sum(n8, axis=0)
        pstarts = jnp.cumsum(tot) - tot
        gbase = pstarts[None, :] + jnp.cumsum(n8, axis=0) - n8
        loff = jnp.cumsum(n8, axis=1) - n8
        n8_flat, loff_flat, gbase_flat = [_smem_table(a) for a in (n8, loff, gbase)]
        nch = (tot + CH_EXP - 1) // CH_EXP
        gend = jnp.cumsum(nch)
        gstart = gend - nch
        gidx = jnp.arange(max_chunks, dtype=I32)
        owner = (gend[None, :] <= gidx[:, None]).astype(I32).sum(axis=1)
        onehot = (owner[:, None] == jnp.arange(N_EXPERTS)[None, :]).astype(I32)
        pick = lambda v: jnp.sum(onehot * v[None, :], axis=1)
        within = (gidx - pick(gstart)) * CH_EXP
        crows = jnp.clip(pick(tot) - within, 0, CH_EXP).astype(I32)
        cstart = (pick(pstarts) + within).astype(I32)
        total = gend[-1].astype(I32).reshape(1)

        xs = _dispatch(n8_flat, loff_flat, gbase_flat, h2b, loc_t, n_rows)
        ys = _experts(gstart.astype(I32), nch.astype(I32), cstart, crows, total, xs,
                      w_gate, w_up, w_down, l)
        x2 = _combine(n8_flat, loff_flat, gbase_flat, ys, loc_t.T, wts_t.T, h2b, x1, gate2,
                      ws_gate[l].astype(BF16), ws_up[l].astype(BF16), ws_down[l].astype(BF16),
                      ln2_g[l].reshape(1, -1), ln2_b[l].reshape(1, -1), B, S, alpha)
    return x2.reshape(B, S, D)
```

```python
import functools

import jax
import jax.numpy as jnp
from jax import lax
from jax.experimental import pallas as pl
from jax.experimental.pallas import tpu as pltpu

F32 = jnp.float32
BF16 = jnp.bfloat16
I32 = jnp.int32
I16 = jnp.int16
U32 = jnp.uint32

GRID_W = 64
N_Q_HEADS = 8
N_KV_HEADS = 2
HEAD_DIM = 64
ROPE_THETA = 10000.0
ATTN_WIDTH = N_Q_HEADS * HEAD_DIM
KV_WIDTH = N_KV_HEADS * HEAD_DIM
SGU_GROUPS = 8
SGU_WIDTH = 512
SGU_GROUP_DIM = SGU_WIDTH // SGU_GROUPS
SGU_CHUNK = 128
N_EXPERTS = 64
TOP_K = 8
N_GROUPS = 8
TOPK_GROUPS = 4
GROUP_SIZE = N_EXPERTS // N_GROUPS
ROUTED_SCALE = 2.5
LN_EPS = 1e-5
RMS_EPS = 1e-6

LANES = 128
SUBLANES = 8
VMEM_LIMIT = 56 * 1024 * 1024

TM_IN = 512
TQ = 256
KV_CHUNK = 512
LOG2_E = 1.4426950408889634
TM_ROUTE = 256
TM_MIX = 512
CH_EXP = 512
CH_CLASS = 128
PERM_BLOCK = 256


def _params(n_axes):
    return pltpu.CompilerParams(
        dimension_semantics=("arbitrary",) * n_axes, vmem_limit_bytes=VMEM_LIMIT)


def _dot(a, b):
    return jnp.dot(a, b, preferred_element_type=F32)


def _dot_nt(a, b):
    return lax.dot_general(a, b, (((1,), (1,)), ((), ())), preferred_element_type=F32)


def _split_bf16(a):
    hi = a.astype(BF16)
    lo = (a - hi.astype(F32)).astype(BF16)
    return hi, lo


def _layer_norm(r, g, b):
    mu = jnp.mean(r, axis=-1, keepdims=True)
    d = r - mu
    var = jnp.mean(d * d, axis=-1, keepdims=True)
    return d * lax.rsqrt(var + LN_EPS) * g + b


def _pack_pairs(y):
    n = y.shape[1] // 2
    hi = lax.bitcast_convert_type(y[:, :n].astype(BF16).astype(F32), U32)
    lo = lax.bitcast_convert_type(y[:, n:].astype(BF16).astype(F32), U32)
    return hi | (lo >> 16)


def _unpack_pairs(w):
    hi = lax.bitcast_convert_type(w & jnp.uint32(0xFFFF0000), F32).astype(BF16)
    lo = lax.bitcast_convert_type(w << 16, F32).astype(BF16)
    return jnp.concatenate([hi, lo], axis=1)


def _mod_kernel(c_ref, w_ref, b_ref, o_ref):
    c = c_ref[...]
    cond = c * jax.nn.sigmoid(c)
    o_ref[...] = _dot(cond, w_ref[...]) + b_ref[...]


def _modulation(c, w_ada, b_ada):
    L, D, D6 = w_ada.shape
    B = c.shape[0]
    nj = D6 // D
    return pl.pallas_call(
        _mod_kernel,
        grid=(L, nj),
        in_specs=[
            pl.BlockSpec((B, D), lambda l, j: (0, 0)),
            pl.BlockSpec((None, D, D), lambda l, j: (l, 0, j)),
            pl.BlockSpec((None, 1, D), lambda l, j: (l, 0, j)),
        ],
        out_specs=pl.BlockSpec((None, B, D), lambda l, j: (l, 0, j)),
        out_shape=jax.ShapeDtypeStruct((L, B, D6), F32),
        compiler_params=_params(2),
        name="modulation",
    )(c, w_ada, b_ada.reshape(L, 1, D6))


def _swap_pairs(a):
    n = a.shape[1]
    lane = lax.broadcasted_iota(I32, a.shape, 1)
    nxt = pltpu.roll(a, n - 1, axis=1)
    prv = pltpu.roll(a, 1, axis=1)
    return jnp.where((lane & 1) == 0, nxt, prv)


def _head_rms(p, gsum_ref, width):
    sq = p * p
    hi, lo = _split_bf16(sq)
    g = gsum_ref[0:width, 0:width]
    ss = _dot(hi, g) + _dot(lo, g)
    return lax.rsqrt(ss * (1.0 / HEAD_DIM) + RMS_EPS)


def _inproj_kernel(x_ref, sc_ref, sh_ref, w_ref, cos_ref, sin_ref, qs_ref, ks_ref, gsum_ref,
                   q_ref, kt_ref, vt_ref, r_ref):
    x = x_ref[...]
    h = (x * (1.0 + sc_ref[0]) + sh_ref[0]).astype(BF16)
    cos2 = cos_ref[...]
    sin2 = sin_ref[...]
    pq = _dot(h, w_ref[:, 0:ATTN_WIDTH])
    qn = pq * _head_rms(pq, gsum_ref, ATTN_WIDTH) * qs_ref[...]
    cos8 = jnp.concatenate([cos2] * (ATTN_WIDTH // LANES), axis=1)
    sin8 = jnp.concatenate([sin2] * (ATTN_WIDTH // LANES), axis=1)
    qr = (qn * cos8 + _swap_pairs(qn) * sin8) * (HEAD_DIM ** -0.5 * LOG2_E)
    for p in range(ATTN_WIDTH // LANES):
        q_ref[p] = qr[:, p * LANES:(p + 1) * LANES].astype(BF16)
    k0 = ATTN_WIDTH
    pk = _dot(h, w_ref[:, k0:k0 + KV_WIDTH])
    kn = pk * _head_rms(pk, gsum_ref, KV_WIDTH) * ks_ref[...]
    kr = kn * cos2 + _swap_pairs(kn) * sin2
    kt_ref[...] = kr.T.astype(BF16)
    v0 = k0 + KV_WIDTH
    pv = _dot(h, w_ref[:, v0:v0 + KV_WIDTH])
    vt_ref[...] = pv.T.astype(BF16)
    r0 = v0 + KV_WIDTH
    n_rest = r_ref.shape[1]
    step = 1024
    for j in range(n_rest // step):
        r_ref[:, j * step:(j + 1) * step] = _dot(
            h, w_ref[:, r0 + j * step:r0 + (j + 1) * step]).astype(BF16)


def _inproj(x2, scale1, shift1, w_in_b, cos2, sin2, qs, ks, gsum, B, S):
    T, D = x2.shape
    tm = min(TM_IN, S)
    npb = S // tm
    n_rest = w_in_b.shape[1] - ATTN_WIDTH - 2 * KV_WIDTH
    nq = ATTN_WIDTH // LANES
    const = lambda i: (0, 0)
    return pl.pallas_call(
        _inproj_kernel,
        grid=(T // tm,),
        in_specs=[
            pl.BlockSpec((tm, D), lambda i: (i, 0)),
            pl.BlockSpec((1, 1, D), lambda i: (i // npb, 0, 0)),
            pl.BlockSpec((1, 1, D), lambda i: (i // npb, 0, 0)),
            pl.BlockSpec(w_in_b.shape, const),
            pl.BlockSpec((tm, LANES), lambda i: (i % npb, 0)),
            pl.BlockSpec((tm, LANES), lambda i: (i % npb, 0)),
            pl.BlockSpec((1, ATTN_WIDTH), const),
            pl.BlockSpec((1, KV_WIDTH), const),
            pl.BlockSpec(gsum.shape, const),
        ],
        out_specs=[
            pl.BlockSpec((nq, tm, LANES), lambda i: (0, i, 0)),
            pl.BlockSpec((None, KV_WIDTH, tm), lambda i: (i // npb, 0, i % npb)),
            pl.BlockSpec((None, KV_WIDTH, tm), lambda i: (i // npb, 0, i % npb)),
            pl.BlockSpec((tm, n_rest), lambda i: (i, 0)),
        ],
        out_shape=[
            jax.ShapeDtypeStruct((nq, T, LANES), BF16),
            jax.ShapeDtypeStruct((B, KV_WIDTH, S), BF16),
            jax.ShapeDtypeStruct((B, KV_WIDTH, S), BF16),
            jax.ShapeDtypeStruct((T, n_rest), BF16),
        ],
        compiler_params=_params(1),
        name="inproj",
    )(x2, scale1, shift1, w_in_b, cos2, sin2, qs, ks, gsum)


def _attn_kernel(q_ref, kt_ref, vt_ref, o_ref, *s_refs):
    kt = kt_ref[...]
    vt = vt_ref[...]
    tq, S = s_refs[0].shape
    kc = min(KV_CHUNK, S)
    heads = []
    for pp in range(q_ref.shape[0]):
        q2 = q_ref[pp]
        for hh in range(LANES // HEAD_DIM):
            heads.append(q2[:, hh * HEAD_DIM:(hh + 1) * HEAD_DIM])
    vt1 = jnp.concatenate([vt, jnp.ones_like(vt)], axis=0)
    s_refs[0][...] = _dot(heads[0], kt)
    outs = []
    for h, s_ref in enumerate(s_refs):
        if h + 1 < len(heads):
            s_refs[h + 1][...] = _dot(heads[h + 1], kt)
        m = jnp.max(s_ref[...], axis=-1, keepdims=True)
        acc = jnp.zeros((tq, 2 * HEAD_DIM), F32)
        for c in range(S // kc):
            p = jnp.exp2(s_ref[:, c * kc:(c + 1) * kc] - m)
            acc = acc + _dot_nt(p.astype(BF16), vt1[:, c * kc:(c + 1) * kc])
        outs.append(acc[:, :HEAD_DIM] / acc[:, HEAD_DIM:])
    o_ref[...] = jnp.concatenate(outs, axis=1).astype(BF16)


def _attention(q4, kt, vt, B, S):
    nq, T, _ = q4.shape
    tq = min(TQ, S)
    nqb = S // tq
    pairs_per_kv = nq // N_KV_HEADS
    heads_per_kv = N_Q_HEADS // N_KV_HEADS
    return pl.pallas_call(
        _attn_kernel,
        grid=(B, N_KV_HEADS, nqb),
        in_specs=[
            pl.BlockSpec((pairs_per_kv, tq, LANES), lambda b, g, i: (g, b * nqb + i, 0)),
            pl.BlockSpec((None, HEAD_DIM, S), lambda b, g, i: (b, g, 0)),
            pl.BlockSpec((None, HEAD_DIM, S), lambda b, g, i: (b, g, 0)),
        ],
        out_specs=pl.BlockSpec((tq, pairs_per_kv * LANES), lambda b, g, i: (b * nqb + i, g)),
        out_shape=jax.ShapeDtypeStruct((T, ATTN_WIDTH), BF16),
        scratch_shapes=[pltpu.VMEM((tq, S), F32)] * heads_per_kv,
        compiler_params=_params(3),
        name="attention",
    )(q4, kt, vt)


def _gelu(a):
    return 0.5 * a * (1.0 + lax.erf(a * (2.0 ** -0.5)))


def _route(sel_t, scores_t):
    tm = sel_t.shape[1]
    neg = -jnp.inf
    row8 = lax.broadcasted_iota(I32, (GROUP_SIZE, tm), 0)
    groups = [sel_t[g * GROUP_SIZE:(g + 1) * GROUP_SIZE, :] for g in range(N_GROUPS)]
    gscore = []
    for a in groups:
        m1 = jnp.max(a, axis=0, keepdims=True)
        i1 = jnp.min(jnp.where(a == m1, row8, GROUP_SIZE), axis=0, keepdims=True)
        m2 = jnp.max(jnp.where(row8 == i1, neg, a), axis=0, keepdims=True)
        gscore.append(m1 + m2)
    masked = []
    for g in range(N_GROUPS):
        rank = jnp.zeros((1, tm), I32)
        for g2 in range(N_GROUPS):
            if g2 == g:
                continue
            if g2 < g:
                ahead = gscore[g2] >= gscore[g]
            else:
                ahead = gscore[g2] > gscore[g]
            rank = rank + ahead.astype(I32)
        keep = rank < TOPK_GROUPS
        masked.append(jnp.where(keep, groups[g], neg))
    cur = jnp.concatenate(masked, axis=0)
    row = lax.broadcasted_iota(I32, (N_EXPERTS, tm), 0)
    wts, hits = [], []
    chosen = jnp.zeros((N_EXPERTS, tm), F32)
    for _ in range(TOP_K):
        m = jnp.max(cur, axis=0, keepdims=True)
        idx = jnp.min(jnp.where(cur == m, row, N_EXPERTS), axis=0, keepdims=True)
        hit = row == idx
        wts.append(jnp.sum(jnp.where(hit, scores_t, 0.0), axis=0, keepdims=True))
        hits.append(hit)
        chosen = jnp.where(hit, 1.0, chosen)
        cur = jnp.where(hit, neg, cur)
    return wts, hits, chosen


def _mix_kernel(z_ref, ga_ref, gb_ref, oa_ref, x_ref, g1_ref, sc2_ref, sh2_ref,
                sg_ref, sb_ref, wcat_ref, bexp_ref, wa_ref, wb_ref, wo_ref,
                l1g_ref, l1b_ref, rwh_ref, rwl_ref, rb_ref, tri_ref, etri_ref,
                x1_ref, h2_ref, loc_ref, wts_ref, n8_ref, *, alpha):
    row_slices = [slice(r0, r0 + TM_ROUTE) for r0 in range(0, z_ref.shape[0], TM_ROUTE)]
    h2s = [_mix_dense(rs, z_ref, ga_ref, gb_ref, oa_ref, x_ref, g1_ref, sc2_ref, sh2_ref,
                      sg_ref, sb_ref, wcat_ref, bexp_ref, wa_ref, wb_ref, wo_ref,
                      l1g_ref, l1b_ref, x1_ref, h2_ref, alpha) for rs in row_slices]
    for sub, rs in enumerate(row_slices):
        _mix_route(sub, rs, h2s[sub], rwh_ref, rwl_ref, rb_ref, tri_ref, etri_ref,
                   loc_ref, wts_ref, n8_ref)


def _mix_dense(rs, z_ref, ga_ref, gb_ref, oa_ref, x_ref, g1_ref, sc2_ref, sh2_ref,
               sg_ref, sb_ref, wcat_ref, bexp_ref, wa_ref, wb_ref, wo_ref,
               l1g_ref, l1b_ref, x1_ref, h2_ref, alpha):
    tm = TM_ROUTE
    z = _gelu(z_ref[rs, :].astype(F32))
    u = z[:, :SGU_WIDTH]
    v = _layer_norm(z[:, SGU_WIDTH:], sg_ref[...], sb_ref[...])
    lane = lax.broadcasted_iota(I32, (SGU_CHUNK, SGU_WIDTH), 1)
    wcat = wcat_ref[...]
    svs = []
    for c in range(tm // SGU_CHUNK):
        vc = v[c * SGU_CHUNK:(c + 1) * SGU_CHUNK, :]
        stacked = jnp.concatenate(
            [jnp.where(lane // SGU_GROUP_DIM == g, vc, 0.0) for g in range(SGU_GROUPS)],
            axis=0).astype(BF16)
        svs.append(_dot(wcat, stacked) + bexp_ref[...])
    sv = jnp.concatenate(svs, axis=0) if len(svs) > 1 else svs[0]
    ob = (u * sv).astype(BF16)
    ya = _dot(oa_ref[rs, :], wa_ref[...])
    yb = _dot(ob, wb_ref[...])
    y = (jax.nn.sigmoid(ga_ref[rs, :].astype(F32)) * ya
         + jax.nn.sigmoid(gb_ref[rs, :].astype(F32)) * yb)
    out = _dot(y.astype(BF16), wo_ref[...])
    x1 = _layer_norm(alpha * x_ref[rs, :] + g1_ref[0] * out, l1g_ref[...], l1b_ref[...])
    x1_ref[rs, :] = x1
    h2 = x1 * (1.0 + sc2_ref[0]) + sh2_ref[0]
    h2_ref[rs, :] = h2.astype(BF16)
    return h2


def _mix_route(sub, rs, h2, rwh_ref, rwl_ref, rb_ref, tri_ref, etri_ref, loc_ref, wts_ref, n8_ref):
    hh, hl = _split_bf16(h2)
    rwh = rwh_ref[...]
    logits_t = _dot_nt(rwh, hh) + _dot_nt(rwh, hl) + _dot_nt(rwl_ref[...], hh)
    scores_t = jax.nn.sigmoid(logits_t)
    sel_t = scores_t + rb_ref[:, 0:1]
    wts, hits, chosen = _route(sel_t, scores_t)
    wsum = wts[0]
    for w in wts[1:]:
        wsum = wsum + w
    inv = ROUTED_SCALE / wsum
    rank = _dot(chosen.astype(BF16), tri_ref[...])
    n = jnp.sum(chosen, axis=1, keepdims=True)
    n8 = jnp.floor((n + (SUBLANES - 1)) * (1.0 / SUBLANES)) * SUBLANES
    n8_wide = jnp.broadcast_to(n8, (N_EXPERTS, LANES))
    run_off = _dot(etri_ref[...], n8_wide.astype(BF16))
    slot = rank + run_off[:, 0:1]
    locs = [jnp.sum(jnp.where(h, slot, 0.0), axis=0, keepdims=True) for h in hits]
    loc_ref[:, rs] = jnp.concatenate(locs, axis=0).astype(I32)
    wts_ref[:, rs] = jnp.concatenate([w * inv for w in wts], axis=0)
    n8_ref[sub] = n8_wide.astype(I32)


def _mix(rest, oa, x2, gate1, scale2, shift2, sg, sb, wcat, bexp, wa, wb, wo, l1g, l1b,
         rwh, rwl, rb, tri, etri, B, S, alpha):
    T, D = x2.shape
    tm = min(TM_MIX, S)
    nsub = tm // TM_ROUTE
    npb = S // tm
    const = lambda i: (0, 0)
    modspec = pl.BlockSpec((1, 1, D), lambda i: (i // npb, 0, 0))
    full = lambda a: pl.BlockSpec(a.shape, const)
    return pl.pallas_call(
        functools.partial(_mix_kernel, alpha=alpha),
        grid=(T // tm,),
        in_specs=[
            pl.BlockSpec((tm, D), lambda i: (i, 0)),
            pl.BlockSpec((tm, D), lambda i: (i, 1)),
            pl.BlockSpec((tm, D), lambda i: (i, 2)),
            pl.BlockSpec((tm, ATTN_WIDTH), lambda i: (i, 0)),
            pl.BlockSpec((tm, D), lambda i: (i, 0)),
            modspec, modspec, modspec,
            full(sg), full(sb), full(wcat), full(bexp), full(wa), full(wb), full(wo),
            full(l1g), full(l1b), full(rwh), full(rwl), full(rb), full(tri), full(etri),
        ],
        out_specs=[
            pl.BlockSpec((tm, D), lambda i: (i, 0)),
            pl.BlockSpec((tm, D), lambda i: (i, 0)),
            pl.BlockSpec((TOP_K, tm), lambda i: (0, i)),
            pl.BlockSpec((TOP_K, tm), lambda i: (0, i)),
            pl.BlockSpec((nsub, N_EXPERTS, LANES), lambda i: (i, 0, 0)),
        ],
        out_shape=[
            jax.ShapeDtypeStruct((T, D), F32),
            jax.ShapeDtypeStruct((T, D), BF16),
            jax.ShapeDtypeStruct((TOP_K, T), I32),
            jax.ShapeDtypeStruct((TOP_K, T), F32),
            jax.ShapeDtypeStruct((T // TM_ROUTE, N_EXPERTS, LANES), I32),
        ],
        compiler_params=_params(1),
        name="mix",
    )(rest, rest, rest, oa, x2, gate1, scale2, shift2, sg, sb, wcat, bexp, wa, wb, wo,
      l1g, l1b, rwh, rwl, rb, tri, etri)


SMEM_ROW = 128


def _smem_table(a):
    a = a.astype(I32)
    a = jnp.concatenate([a, jnp.sum(a, axis=1, keepdims=True)], axis=1)
    return jnp.pad(a, ((0, 0), (0, SMEM_ROW - a.shape[1]))).reshape(-1)


def _wait_all_runs(n_ref, vmem_ref, hbm_ref, sem):
    total = pl.multiple_of(n_ref[N_EXPERTS], SUBLANES)
    _run_copy(vmem_ref, 0, hbm_ref, 0, total, sem).wait()


def _tile_rows(tm):
    return tm * TOP_K + N_EXPERTS * SUBLANES


def _run_copy(src_ref, s, dst_ref, d, n, sem):
    return pltpu.make_async_copy(src_ref.at[pl.ds(s, n)], dst_ref.at[pl.ds(d, n)], sem)


def _for_each_run(n_ref, fn):
    for e in range(N_EXPERTS):
        n = pl.multiple_of(n_ref[e], SUBLANES)

        @pl.when(n > 0)
        def _():
            fn(e, n)


def _dispatch_kernel(n8_ref, loff_ref, gbase_ref, n8_prev_ref, h_ref, loc_ref,
                     xs_ref, xp_ref, sems):
    i = pl.program_id(0)
    rows, tm = xp_ref.shape[1], h_ref.shape[0]
    slot = i % 2

    loc16 = loc_ref[...].astype(I16)
    hb = h_ref[...]
    half = hb.shape[1] // 2
    for r0 in range(0, rows, PERM_BLOCK):
        row = (lax.broadcasted_iota(I32, (PERM_BLOCK, tm), 0) + r0).astype(I16)
        perm = jnp.zeros((PERM_BLOCK, tm), BF16)
        one = jnp.ones((PERM_BLOCK, tm), BF16)
        for k in range(TOP_K):
            perm = jnp.where(row == loc16[k:k + 1, :], one, perm)
        xp = _dot(perm, hb)
        xp_ref[slot, r0:r0 + PERM_BLOCK, :] = (
            lax.bitcast_convert_type(xp[:, :half], U32)
            | (lax.bitcast_convert_type(xp[:, half:], U32) >> 16))

    def run_copy(e, n):
        return _run_copy(xp_ref.at[slot], pl.multiple_of(loff_ref[e], SUBLANES),
                         xs_ref, pl.multiple_of(gbase_ref[e], SUBLANES), n, sems.at[slot])

    _for_each_run(n8_ref, lambda e, n: run_copy(e, n).start())

    @pl.when(i > 0)
    def _():
        _wait_all_runs(n8_prev_ref, xp_ref.at[1 - slot], xs_ref, sems.at[1 - slot])

    @pl.when(i == pl.num_programs(0) - 1)
    def _():
        _wait_all_runs(n8_ref, xp_ref.at[slot], xs_ref, sems.at[slot])


def _dispatch(n8_flat, loff_flat, gbase_flat, h2b, loc_t, n_rows):
    T, D = h2b.shape
    tm = min(TM_ROUTE, T)
    rows = _tile_rows(tm)
    per_tile = pl.BlockSpec((SMEM_ROW,), lambda i: (i,), memory_space=pltpu.SMEM)
    prev_tile = pl.BlockSpec((SMEM_ROW,), lambda i: (jnp.maximum(i - 1, 0),),
                             memory_space=pltpu.SMEM)
    return pl.pallas_call(
        _dispatch_kernel,
        grid=(T // tm,),
        in_specs=[
            per_tile, per_tile, per_tile, prev_tile,
            pl.BlockSpec((tm, D), lambda i: (i, 0)),
            pl.BlockSpec((TOP_K, tm), lambda i: (0, i)),
        ],
        out_specs=pl.BlockSpec(memory_space=pl.ANY),
        out_shape=jax.ShapeDtypeStruct((n_rows, D // 2), U32),
        scratch_shapes=[pltpu.VMEM((2, rows, D // 2), U32), pltpu.SemaphoreType.DMA((2,))],
        compiler_params=_params(1),
        name="dispatch",
    )(n8_flat, loff_flat, gbase_flat, n8_flat, h2b, loc_t)


N_XBUF = 3
N_YBUF = 2


def _experts_kernel(gstart_ref, nch_ref, cstart_ref, crows_ref, total_ref,
                    xs_ref, wg_ref, wu_ref, wd_ref, ys_ref,
                    xbuf, ybuf, wg_s, wu_s, wd_s, sem_in, sem_out):
    e = pl.program_id(0)
    total = total_ref[0]

    def in_copy(g):
        s = g % N_XBUF
        n = pl.multiple_of(crows_ref[g], SUBLANES)
        return _run_copy(xs_ref, pl.multiple_of(cstart_ref[g], SUBLANES), xbuf.at[s], 0, n,
                         sem_in.at[s])

    def out_copy(g):
        s = g % N_YBUF
        n = pl.multiple_of(crows_ref[g], SUBLANES)
        return _run_copy(ybuf.at[s], 0, ys_ref, pl.multiple_of(cstart_ref[g], SUBLANES), n,
                         sem_out.at[s])

    def when_chunk(g, fn):
        @pl.when(jnp.logical_and(g >= 0, g < total))
        def _():
            fn(g)

    @pl.when(e == 0)
    def _():
        xbuf[...] = jnp.zeros_like(xbuf)
        for g0 in range(N_XBUF - 1):
            when_chunk(g0, lambda g: in_copy(g).start())

    wg_s[...] = wg_ref[...].astype(BF16)
    wu_s[...] = wu_ref[...].astype(BF16)
    wd_s[...] = wd_ref[...].astype(BF16)

    def chunk(c, carry):
        g = gstart_ref[e] + c
        in_copy(g).wait()
        when_chunk(g + (N_XBUF - 1), lambda g2: in_copy(g2).start())
        when_chunk(g - N_YBUF, lambda g2: out_copy(g2).wait())
        xs_slot = xbuf.at[g % N_XBUF]
        ys_slot = ybuf.at[g % N_YBUF]
        rows = crows_ref[g]
        for m in range(CH_CLASS, CH_EXP + 1, CH_CLASS):
            @pl.when(jnp.logical_and(rows > m - CH_CLASS, rows <= m))
            def _():
                xb = _unpack_pairs(xs_slot[0:m, :])
                gate = _dot(xb, wg_s[...])
                up = _dot(xb, wu_s[...])
                hb = (gate * jax.nn.sigmoid(gate) * up).astype(BF16)
                ys_slot[0:m, :] = _pack_pairs(_dot(hb, wd_s[...]))
        out_copy(g).start()
        return carry
    lax.fori_loop(0, nch_ref[e], chunk, 0)

    @pl.when(e == pl.num_programs(0) - 1)
    def _():
        for back in range(N_YBUF, 0, -1):
            when_chunk(total - back, lambda g: out_copy(g).wait())


def _experts(gstart, nch, cstart, crows, total, xs, w_gate, w_up, w_down, layer):
    _, E, D, FF = w_gate.shape
    wspec = lambda a, b: pl.BlockSpec((None, None, a, b), lambda e, *_: (layer, e, 0, 0))
    grid_spec = pltpu.PrefetchScalarGridSpec(
        num_scalar_prefetch=5,
        grid=(E,),
        in_specs=[pl.BlockSpec(memory_space=pl.ANY), wspec(D, FF), wspec(D, FF), wspec(FF, D)],
        out_specs=pl.BlockSpec(memory_space=pl.ANY),
        scratch_shapes=[
            pltpu.VMEM((N_XBUF, CH_EXP, D // 2), U32),
            pltpu.VMEM((N_YBUF, CH_EXP, D // 2), U32),
            pltpu.VMEM((D, FF), BF16),
            pltpu.VMEM((D, FF), BF16),
            pltpu.VMEM((FF, D), BF16),
            pltpu.SemaphoreType.DMA((N_XBUF,)),
            pltpu.SemaphoreType.DMA((N_YBUF,)),
        ],
    )
    return pl.pallas_call(
        _experts_kernel,
        grid_spec=grid_spec,
        out_shape=jax.ShapeDtypeStruct(xs.shape, U32),
        compiler_params=_params(1),
        name="experts",
    )(gstart, nch, cstart, crows, total, xs, w_gate, w_up, w_down)


def _combine_kernel(n8_ref, loff_ref, gbase_ref, n8_nxt_ref, loff_nxt_ref, gbase_nxt_ref,
                    ys_ref, loc_ref, w_ref, h_ref, x1_ref, g2_ref,
                    wsg_ref, wsu_ref, wsd_ref, l2g_ref, l2b_ref, o_ref, yp_ref, sems, *, alpha):
    i = pl.program_id(0)
    rows, tm = yp_ref.shape[1], h_ref.shape[0]
    slot = i % 2

    def fetch(lo_ref, gb_ref, s):
        def copy(e, n):
            return _run_copy(ys_ref, pl.multiple_of(gb_ref[e], SUBLANES),
                             yp_ref.at[s], pl.multiple_of(lo_ref[e], SUBLANES), n, sems.at[s])
        return copy

    @pl.when(i == 0)
    def _():
        yp_ref[...] = jnp.zeros_like(yp_ref)
        cur = fetch(loff_ref, gbase_ref, slot)
        _for_each_run(n8_ref, lambda e, n: cur(e, n).start())

    @pl.when(i + 1 < pl.num_programs(0))
    def _():
        nxt = fetch(loff_nxt_ref, gbase_nxt_ref, 1 - slot)
        _for_each_run(n8_nxt_ref, lambda e, n: nxt(e, n).start())

    hb = h_ref[...]
    g = _dot(hb, wsg_ref[...])
    u = _dot(hb, wsu_ref[...])
    y = _dot((g * jax.nn.sigmoid(g) * u).astype(BF16), wsd_ref[...])

    loc = loc_ref[...]
    w = w_ref[...]
    pws = []
    for r0 in range(0, rows, PERM_BLOCK):
        lane = (lax.broadcasted_iota(I32, (tm, PERM_BLOCK), 1) + r0).astype(I16)
        pw = jnp.zeros((tm, PERM_BLOCK), BF16)
        for k in range(TOP_K):
            lk = jnp.broadcast_to(loc[:, k:k + 1], (tm, PERM_BLOCK)).astype(I16)
            wk = jnp.broadcast_to(w[:, k:k + 1], (tm, PERM_BLOCK)).astype(BF16)
            pw = jnp.where(lane == lk, wk, pw)
        pws.append(pw)

    _wait_all_runs(n8_ref, yp_ref.at[slot], ys_ref, sems.at[slot])

    for j, r0 in enumerate(range(0, rows, PERM_BLOCK)):
        yp = _unpack_pairs(yp_ref[slot, r0:r0 + PERM_BLOCK, :])
        y = y + _dot(pws[j], yp)
    o_ref[...] = _layer_norm(alpha * x1_ref[...] + g2_ref[0] * y, l2g_ref[...], l2b_ref[...])


def _combine(n8_flat, loff_flat, gbase_flat, ys, loc_tk, w_tk, h2b, x1, gate2, wsg, wsu, wsd,
             l2g, l2b, B, S, alpha):
    T, D = h2b.shape
    tm = min(TM_ROUTE, S)
    npb = S // tm
    rows = _tile_rows(tm)
    const = lambda i: (0, 0)
    full = lambda a: pl.BlockSpec(a.shape, const)
    n_tiles = T // tm
    per_tile = pl.BlockSpec((SMEM_ROW,), lambda i: (i,), memory_space=pltpu.SMEM)
    next_tile = pl.BlockSpec((SMEM_ROW,), lambda i: (jnp.minimum(i + 1, n_tiles - 1),),
                             memory_space=pltpu.SMEM)
    return pl.pallas_call(
        functools.partial(_combine_kernel, alpha=alpha),
        grid=(n_tiles,),
        in_specs=[
            per_tile, per_tile, per_tile, next_tile, next_tile, next_tile,
            pl.BlockSpec(memory_space=pl.ANY),
            pl.BlockSpec((tm, TOP_K), lambda i: (i, 0)),
            pl.BlockSpec((tm, TOP_K), lambda i: (i, 0)),
            pl.BlockSpec((tm, D), lambda i: (i, 0)),
            pl.BlockSpec((tm, D), lambda i: (i, 0)),
            pl.BlockSpec((1, 1, D), lambda i: (i // npb, 0, 0)),
            full(wsg), full(wsu), full(wsd), full(l2g), full(l2b),
        ],
        out_specs=pl.BlockSpec((tm, D), lambda i: (i, 0)),
        out_shape=jax.ShapeDtypeStruct((T, D), F32),
        scratch_shapes=[pltpu.VMEM((2, rows, D // 2), U32), pltpu.SemaphoreType.DMA((2,))],
        compiler_params=_params(1),
        name="combine",
    )(n8_flat, loff_flat, gbase_flat, n8_flat, loff_flat, gbase_flat,
      ys, loc_tk, w_tk, h2b, x1, gate2, wsg, wsu, wsd, l2g, l2b)


def _rope_tables(S):
    rows = S // GRID_W
    t = jnp.arange(S)
    row = (t // GRID_W - rows // 2).astype(F32)
    col = (t % GRID_W - GRID_W // 2).astype(F32)
    n_pairs = HEAD_DIM // 4
    inv_freq = ROPE_THETA ** (-jnp.arange(n_pairs, dtype=F32) / n_pairs)
    ang = jnp.concatenate([row[:, None] * inv_freq, col[:, None] * inv_freq], -1)
    cos, sin = jnp.cos(ang), jnp.sin(ang)
    cos_i = jnp.repeat(cos, 2, axis=-1)
    sin_i = jnp.stack([-sin, sin], axis=-1).reshape(S, HEAD_DIM)
    reps = LANES // HEAD_DIM
    return jnp.tile(cos_i, (1, reps)), jnp.tile(sin_i, (1, reps))


def kernel(x, c, w_ada, b_ada, w_in, q_scale, k_scale, sgu_ln_g, sgu_ln_b, w_spatial, b_spatial,
           w_branch_a, w_branch_b, w_out, ln1_g, ln1_b, router_w, router_bias, w_gate, w_up, w_down,
           ws_gate, ws_up, ws_down, ln2_g, ln2_b):
    B, S, D = x.shape
    T = B * S
    L = w_ada.shape[0]
    alpha = (2 * L) ** 0.25
    tm = min(TM_ROUTE, S)
    n_tiles = T // tm
    n_rows = T * TOP_K + n_tiles * N_EXPERTS * SUBLANES
    max_chunks = n_rows // CH_EXP + N_EXPERTS

    cos2, sin2 = _rope_tables(S)
    mod = _modulation(c, w_ada, b_ada)
    hd = jnp.arange(ATTN_WIDTH) // HEAD_DIM
    gsum = (hd[:, None] == hd[None, :]).astype(BF16)
    tri = (jnp.arange(tm)[:, None] < jnp.arange(tm)[None, :]).astype(BF16)
    etri = (jnp.arange(N_EXPERTS)[None, :] < jnp.arange(N_EXPERTS)[:, None]).astype(BF16)

    x2 = x.reshape(T, D)
    for l in range(L):
        m6 = mod[l].reshape(B, 1, 6, D)
        shift1, scale1, gate1, shift2, scale2, gate2 = [m6[:, :, j, :] for j in range(6)]
        qs = jnp.tile(q_scale[l], N_Q_HEADS).reshape(1, ATTN_WIDTH)
        ks = jnp.tile(k_scale[l], N_KV_HEADS).reshape(1, KV_WIDTH)
        q4, kt, vt, rest = _inproj(x2, scale1, shift1, w_in[l].astype(BF16), cos2, sin2, qs, ks,
                                   gsum, B, S)
        oa = _attention(q4, kt, vt, B, S)

        wcat = jnp.transpose(w_spatial[l], (1, 0, 2)).reshape(SGU_CHUNK, SGU_GROUPS * SGU_CHUNK)
        bexp = jnp.repeat(b_spatial[l].T, SGU_GROUP_DIM, axis=1)
        rw_t = router_w[l].T
        rwh = rw_t.astype(BF16)
        rwl = (rw_t - rwh.astype(F32)).astype(BF16)
        rb = jnp.broadcast_to(router_bias[l].astype(F32)[:, None], (N_EXPERTS, LANES))
        x1, h2b, loc_t, wts_t, n8_out = _mix(
            rest, oa, x2, gate1, scale2, shift2,
            sgu_ln_g[l].reshape(1, -1), sgu_ln_b[l].reshape(1, -1), wcat.astype(BF16), bexp,
            w_branch_a[l].astype(BF16), w_branch_b[l].astype(BF16), w_out[l].astype(BF16),
            ln1_g[l].reshape(1, -1), ln1_b[l].reshape(1, -1), rwh, rwl, rb, tri, etri, B, S, alpha)

        n8 = n8_out[:, :, 0]
        tot = jnp.sum(n8, axis=0)
        pstarts = jnp.cumsum(tot) - tot
        gbase = pstarts[None, :] + jnp.cumsum(n8, axis=0) - n8
        loff = jnp.cumsum(n8, axis=1) - n8
        n8_flat, loff_flat, gbase_flat = [_smem_table(a) for a in (n8, loff, gbase)]
        nch = (tot + CH_EXP - 1) // CH_EXP
        gend = jnp.cumsum(nch)
        gstart = gend - nch
        gidx = jnp.arange(max_chunks, dtype=I32)
        owner = (gend[None, :] <= gidx[:, None]).astype(I32).sum(axis=1)
        onehot = (owner[:, None] == jnp.arange(N_EXPERTS)[None, :]).astype(I32)
        pick = lambda v: jnp.sum(onehot * v[None, :], axis=1)
        within = (gidx - pick(gstart)) * CH_EXP
        crows = jnp.clip(pick(tot) - within, 0, CH_EXP).astype(I32)
        cstart = (pick(pstarts) + within).astype(I32)
        total = gend[-1].astype(I32).reshape(1)

        xs = _dispatch(n8_flat, loff_flat, gbase_flat, h2b, loc_t, n_rows)
        ys = _experts(gstart.astype(I32), nch.astype(I32), cstart, crows, total, xs,
                      w_gate, w_up, w_down, l)
        x2 = _combine(n8_flat, loff_flat, gbase_flat, ys, loc_t.T, wts_t.T, h2b, x1, gate2,
                      ws_gate[l].astype(BF16), ws_up[l].astype(BF16), ws_down[l].astype(BF16),
                      ln2_g[l].reshape(1, -1), ln2_b[l].reshape(1, -1), B, S, alpha)
    return x2.reshape(B, S, D)
```
